```python
import functools
import jax
import jax.numpy as jnp
from jax import lax
import numpy as np


D_MODEL = 1024
BATCH = 4
SEQ = 4096
DEPTH = 2
DEC_BATCH = 128
DEC_SEQ = 1
PAST_LEN = 2048
PAGE_SIZE = 128

HEAD_DIM = 64
FOX_W = D_MODEL // 2
DSA_W = D_MODEL // 4
POOL_W = D_MODEL // 4
FOX_HEADS = FOX_W // HEAD_DIM
DSA_HEADS = DSA_W // HEAD_DIM
POOL_WINDOWS = (2, 4, 8, 16)
POOL_GROUPS = len(POOL_WINDOWS)
POOL_CH = POOL_W // POOL_GROUPS
POOL_BUF = POOL_WINDOWS[-1] - 1
IDX_HEADS = 8
IDX_DIM = 32
TOPK_MAX = 256
Q_BLOCK = 128
ROPE_THETA = 500000.0
ROPE_FRACTION = 4
FORGET_BIAS = 3.0
X_HEADS = 4
X_W = X_HEADS * HEAD_DIM
N_MEM = 256
D_FF = 2816
N_EXPERTS = 8
TOP_K = 2
D_FF_EXPERT = 3584
MOE_BLOCK = 128
N_DENSE = (DEPTH + 1) // 2
N_MOE = DEPTH // 2
EPS = 1e-6
IN_SPLITS = (FOX_W, FOX_W, FOX_W, FOX_HEADS, DSA_W, DSA_W, DSA_W, IDX_HEADS * IDX_DIM, IDX_DIM, IDX_HEADS, POOL_W)
IN_W = sum(IN_SPLITS)

kernel_name = 'hybrid_fox_dsa_pool_decoder_step'


def rmsnorm(x, g):
    xf = x.astype(jnp.float32)
    y = xf * lax.rsqrt(jnp.mean(xf * xf, axis=-1, keepdims=True) + EPS)
    return (y * g.astype(jnp.float32)).astype(x.dtype)


def rope_partial(x, pos):
    d = x.shape[-1]
    rot = d // ROPE_FRACTION
    half = rot // 2
    inv = jnp.float32(ROPE_THETA) ** (-jnp.arange(half, dtype=jnp.float32) / half)
    ang = pos.astype(jnp.float32)[:, None] * inv[None, :]
    shape = (pos.shape[0],) + (1,) * (x.ndim - 3) + (half,)
    cos = jnp.cos(ang).reshape(shape)
    sin = jnp.sin(ang).reshape(shape)
    x1 = x[..., :half].astype(jnp.float32)
    x2 = x[..., half:rot].astype(jnp.float32)
    r = jnp.concatenate([x1 * cos - x2 * sin, x2 * cos + x1 * sin, x[..., rot:].astype(jnp.float32)], axis=-1)
    return r.astype(x.dtype)


def _blockify(a, qb, n_blk):
    pad = n_blk * qb - a.shape[1]
    a = jnp.pad(a, [(0, 0), (0, pad)] + [(0, 0)] * (a.ndim - 2))
    a = a.reshape((a.shape[0], n_blk, qb) + a.shape[2:])
    return jnp.moveaxis(a, 1, 0)


def _unblockify(o, n_q):
    o = jnp.moveaxis(o, 0, 1)
    o = o.reshape((o.shape[0], o.shape[1] * o.shape[2]) + o.shape[3:])
    return o[:, :n_q]


def fox_attention(q, k, v, cf_q, cf_k, q_pos, k_pos):
    B, T, H, d = q.shape
    qb = min(Q_BLOCK, T)
    nb = -(-T // qb)
    scale = d ** -0.5
    cfk = jnp.transpose(cf_k, (0, 2, 1))

    def one(args):
        qq, cq, pq = args
        s = jnp.einsum('bqhd,bkhd->bhqk', qq, k, preferred_element_type=jnp.float32) * scale
        s = s + jnp.transpose(cq, (0, 2, 1))[..., None] - cfk[:, :, None, :]
        mask = k_pos[None, :] <= pq[0][:, None]
        s = jnp.where(mask, s, -jnp.inf)
        p = jax.nn.softmax(s, axis=-1)
        return jnp.einsum('bhqk,bkhd->bqhd', p.astype(v.dtype), v)

    o = lax.map(one, (_blockify(q, qb, nb), _blockify(cf_q, qb, nb), _blockify(q_pos[None], qb, nb)))
    return _unblockify(o, T)


def dsa_attention(q, k, v, qi, wi, ki, q_pos, k_pos):
    B, T, H, d = q.shape
    L = k.shape[1]
    n_sel = min(TOPK_MAX, L // 4)
    qb = min(Q_BLOCK, T)
    nb = -(-T // qb)
    scale = d ** -0.5
    take = jax.vmap(lambda a, i: a[i])

    def one(args):
        qq, qiq, wq, pq = args
        rel = jax.nn.relu(jnp.einsum('bqji,bki->bqjk', qiq, ki, preferred_element_type=jnp.float32))
        score = jnp.einsum('bqj,bqjk->bqk', wq.astype(jnp.float32), rel)
        mask = k_pos[None, :] <= pq[0][:, None]
        score = jnp.where(mask[None], score, -jnp.inf)
        top_s, sel = lax.top_k(score, n_sel)
        ok = top_s > -jnp.inf
        ks = take(k, sel)
        vs = take(v, sel)
        s = jnp.einsum('bqhd,bqkhd->bhqk', qq, ks, preferred_element_type=jnp.float32) * scale
        s = jnp.where(ok[:, None], s, -jnp.inf)
        p = jax.nn.softmax(s, axis=-1)
        return jnp.einsum('bhqk,bqkhd->bqhd', p.astype(vs.dtype), vs)

    o = lax.map(one, (_blockify(q, qb, nb), _blockify(qi, qb, nb), _blockify(wi, qb, nb),
                      _blockify(q_pos[None], qb, nb)))
    return _unblockify(o, T)


def multiscale_pool(u, pos, w_pool, pool_scale):
    B, L, _ = u.shape
    ug = u.reshape(B, L, POOL_GROUPS, POOL_CH)
    csum = jnp.cumsum(ug.astype(jnp.float32), axis=1)
    means = []
    for g, w in enumerate(POOL_WINDOWS):
        c = csum[:, :, g]
        prev = jnp.pad(c, ((0, 0), (w, 0), (0, 0)))[:, :L]
        cnt = jnp.minimum(pos + 1, w).astype(jnp.float32)[None, :, None]
        means.append((c - prev) / cnt)
    mean = jnp.stack(means, axis=2)
    dlt = (mean - ug.astype(jnp.float32)).astype(u.dtype)
    y = jnp.einsum('blgc,gce->blge', dlt, w_pool) * pool_scale.reshape(POOL_GROUPS, POOL_CH)
    return y.reshape(B, L, POOL_W)


def token_mixers(h, pos, past, w_in, b_forget, w_pool, pool_scale, w_out):
    B, T, _ = h.shape
    z = h @ w_in
    points = np.cumsum(IN_SPLITS)[:-1].tolist()
    qa, ka, va, fa, qb_, kb, vb, qi, ki, wi, u = jnp.split(z, points, axis=-1)
    qa = qa.reshape(B, T, FOX_HEADS, HEAD_DIM)
    ka = ka.reshape(B, T, FOX_HEADS, HEAD_DIM)
    va = va.reshape(B, T, FOX_HEADS, HEAD_DIM)
    logf = jax.nn.log_sigmoid(fa.astype(jnp.float32) + b_forget.astype(jnp.float32))
    qb_ = rope_partial(qb_.reshape(B, T, DSA_HEADS, HEAD_DIM), pos)
    kb = rope_partial(kb.reshape(B, T, DSA_HEADS, HEAD_DIM), pos)
    vb = vb.reshape(B, T, DSA_HEADS, HEAD_DIM)
    qi = rope_partial(qi.reshape(B, T, IDX_HEADS, IDX_DIM), pos)
    ki = rope_partial(ki, pos)
    if past is None:
        kA, vA, lfA, kB, vB, kI, uu = ka, va, logf, kb, vb, ki, u
        k_pos = pos
        u_pos = pos
    else:
        p_ka, p_va, p_lf, p_kb, p_vb, p_ki, p_u = past
        n_past = p_ka.shape[1]
        kA = jnp.concatenate([p_ka, ka], axis=1)
        vA = jnp.concatenate([p_va, va], axis=1)
        lfA = jnp.concatenate([p_lf.astype(jnp.float32), logf], axis=1)
        kB = jnp.concatenate([p_kb, kb], axis=1)
        vB = jnp.concatenate([p_vb, vb], axis=1)
        kI = jnp.concatenate([p_ki, ki], axis=1)
        uu = jnp.concatenate([p_u, u], axis=1)
        k_pos = jnp.arange(n_past + T, dtype=jnp.int32)
        u_pos = jnp.arange(n_past - p_u.shape[1], n_past + T, dtype=jnp.int32)
    cf = jnp.cumsum(lfA, axis=1)
    oA = fox_attention(qa, kA, vA, cf[:, -T:], cf, pos, k_pos)
    oB = dsa_attention(qb_, kB, vB, qi, wi, kI, pos, k_pos)
    oC = multiscale_pool(uu, u_pos, w_pool, pool_scale)[:, -T:]
    o = jnp.concatenate([oA.reshape(B, T, FOX_W), oB.reshape(B, T, DSA_W), oC.astype(oA.dtype)], axis=-1)
    new_state = (ka, va, logf, kb, vb, ki, uu[:, -POOL_BUF:])
    return o @ w_out, new_state


def memory_kv(mem, g_mem, w_xkv):
    B, M, _ = mem.shape
    kv = rmsnorm(mem, g_mem) @ w_xkv
    k, v = jnp.split(kv, 2, axis=-1)
    return k.reshape(B, M, X_HEADS, HEAD_DIM), v.reshape(B, M, X_HEADS, HEAD_DIM)


def cross_attention(h, mk, mv, w_xq, w_xo):
    B, T, _ = h.shape
    q = (h @ w_xq).reshape(B, T, X_HEADS, HEAD_DIM)
    s = jnp.einsum('bqhd,bmhd->bhqm', q, mk, preferred_element_type=jnp.float32) * (HEAD_DIM ** -0.5)
    p = jax.nn.softmax(s, axis=-1)
    o = jnp.einsum('bhqm,bmhd->bqhd', p.astype(mv.dtype), mv).reshape(B, T, X_W)
    return o @ w_xo


def swiglu(h, w_gate, w_up, w_down):
    return (jax.nn.silu(h @ w_gate) * (h @ w_up)) @ w_down


def moe_swiglu(h, w_router, w_gate, w_up, w_down):
    B, S, D = h.shape
    xt = h.reshape(B * S, D)
    n_as = B * S * TOP_K
    logits = (xt @ w_router).astype(jnp.float32)
    top_logit, top_e = lax.top_k(logits, TOP_K)
    gates = jax.nn.softmax(top_logit, axis=-1)
    flat_e = top_e.reshape(n_as)
    flat_tok = jnp.arange(n_as, dtype=jnp.int32) // TOP_K
    flat_g = gates.reshape(n_as)
    order = jnp.argsort(flat_e)
    se, stok, sg = flat_e[order], flat_tok[order], flat_g[order]
    counts = jnp.bincount(flat_e, length=N_EXPERTS)
    padded = (counts + MOE_BLOCK - 1) // MOE_BLOCK * MOE_BLOCK
    grp_start = jnp.cumsum(counts) - counts
    pad_end = jnp.cumsum(padded)
    pad_start = pad_end - padded
    dest = pad_start[se] + jnp.arange(n_as, dtype=jnp.int32) - grp_start[se]
    n_blocks = -(-n_as // MOE_BLOCK) + N_EXPERTS
    rows = n_blocks * MOE_BLOCK
    row_tok = jnp.zeros((rows,), jnp.int32).at[dest].set(stok)
    blk_e = jnp.minimum(jnp.searchsorted(pad_end, jnp.arange(n_blocks) * MOE_BLOCK, side='right'), N_EXPERTS - 1)
    xb = xt[row_tok].reshape(n_blocks, MOE_BLOCK, D)

    def run(args):
        xblk, e = args
        return (jax.nn.silu(xblk @ w_gate[e]) * (xblk @ w_up[e])) @ w_down[e]

    yb = lax.map(run, (xb, blk_e)).reshape(rows, D)
    contrib = yb[dest] * sg[:, None].astype(yb.dtype)
    out = jnp.zeros_like(xt).at[stok].add(contrib)
    return out.reshape(B, S, D)


def decoder_layer(x, pos, past, mk, mv, lw, ffn):
    g_mix, w_in, b_forget, w_pool, pool_scale, w_out, g_x, w_xq, w_xo, g_ffn = lw
    y, st = token_mixers(rmsnorm(x, g_mix), pos, past, w_in, b_forget, w_pool, pool_scale, w_out)
    x = x + y
    x = x + cross_attention(rmsnorm(x, g_x), mk, mv, w_xq, w_xo)
    x = x + ffn(rmsnorm(x, g_ffn))
    return x, st


def gather_pages(pool, layer, page_table):
    g = pool[layer, page_table]
    return g.reshape((g.shape[0], g.shape[1] * g.shape[2]) + g.shape[3:])


def stack_layers(states, j):
    return jnp.stack([s[j] for s in states], axis=0)


def setup_inputs(seed: int = 0) -> dict:
    key = jax.random.key(seed)
    ks = list(jax.random.split(key, 48))
    nxt = iter(ks)

    def nrm(shape, scale=1.0):
        return jax.random.normal(next(nxt), shape, jnp.float32) * scale

    def gain(shape):
        return 1.0 + 0.1 * nrm(shape)

    n_pages = PAST_LEN // PAGE_SIZE
    n_used = DEC_BATCH * n_pages
    n_pool = (n_used * 5) // 4
    page_table = jax.random.permutation(next(nxt), n_pool)[:n_used].reshape(DEC_BATCH, n_pages).astype(jnp.int32)
    return {
        'x_prompt': nrm((BATCH, SEQ, D_MODEL)),
        'x_sample': nrm((DEC_BATCH, DEC_SEQ, D_MODEL)),
        'cache_fox_k': nrm((DEPTH, n_pool, PAGE_SIZE, FOX_HEADS, HEAD_DIM)),
        'cache_fox_v': nrm((DEPTH, n_pool, PAGE_SIZE, FOX_HEADS, HEAD_DIM)),
        'cache_fox_logf': jax.nn.log_sigmoid(FORGET_BIAS + nrm((DEPTH, n_pool, PAGE_SIZE, FOX_HEADS))),
        'cache_dsa_k': nrm((DEPTH, n_pool, PAGE_SIZE, DSA_HEADS, HEAD_DIM)),
        'cache_dsa_v': nrm((DEPTH, n_pool, PAGE_SIZE, DSA_HEADS, HEAD_DIM)),
        'cache_dsa_kidx': nrm((DEPTH, n_pool, PAGE_SIZE, IDX_DIM)),
        'state_pool': nrm((DEPTH, DEC_BATCH, POOL_BUF, POOL_W)),
        'cache_mem_k': nrm((DEPTH, DEC_BATCH, N_MEM, X_HEADS, HEAD_DIM)),
        'cache_mem_v': nrm((DEPTH, DEC_BATCH, N_MEM, X_HEADS, HEAD_DIM)),
        'page_table': page_table,
        'mem_prompt': nrm((BATCH, N_MEM, D_MODEL)),
        'norm_mix': gain((DEPTH, D_MODEL)),
        'w_in': nrm((DEPTH, D_MODEL, IN_W), D_MODEL ** -0.5),
        'b_forget': FORGET_BIAS + 0.1 * nrm((DEPTH, FOX_HEADS)),
        'w_pool': nrm((DEPTH, POOL_GROUPS, POOL_CH, POOL_CH), POOL_CH ** -0.5),
        'pool_scale': gain((DEPTH, POOL_W)),
        'w_out': nrm((DEPTH, D_MODEL, D_MODEL), D_MODEL ** -0.5),
        'norm_x': gain((DEPTH, D_MODEL)),
        'norm_mem': gain((DEPTH, D_MODEL)),
        'w_xq': nrm((DEPTH, D_MODEL, X_W), D_MODEL ** -0.5),
        'w_xkv': nrm((DEPTH, D_MODEL, 2 * X_W), D_MODEL ** -0.5),
        'w_xo': nrm((DEPTH, X_W, D_MODEL), X_W ** -0.5),
        'norm_ffn': gain((DEPTH, D_MODEL)),
        'ffn_w_gate': nrm((N_DENSE, D_MODEL, D_FF), D_MODEL ** -0.5),
        'ffn_w_up': nrm((N_DENSE, D_MODEL, D_FF), D_MODEL ** -0.5),
        'ffn_w_down': nrm((N_DENSE, D_FF, D_MODEL), D_FF ** -0.5),
        'moe_router': nrm((N_MOE, D_MODEL, N_EXPERTS), D_MODEL ** -0.5),
        'moe_w_gate': nrm((N_MOE, N_EXPERTS, D_MODEL, D_FF_EXPERT), D_MODEL ** -0.5),
        'moe_w_up': nrm((N_MOE, N_EXPERTS, D_MODEL, D_FF_EXPERT), D_MODEL ** -0.5),
        'moe_w_down': nrm((N_MOE, N_EXPERTS, D_FF_EXPERT, D_MODEL), D_FF_EXPERT ** -0.5),
        'norm_final': gain((D_MODEL,)),
    }


def reference(x_prompt, x_sample, cache_fox_k, cache_fox_v, cache_fox_logf, cache_dsa_k, cache_dsa_v,
              cache_dsa_kidx, state_pool, cache_mem_k, cache_mem_v, page_table, mem_prompt,
              norm_mix, w_in, b_forget, w_pool, pool_scale, w_out, norm_x, norm_mem, w_xq, w_xkv, w_xo,
              norm_ffn, ffn_w_gate, ffn_w_up, ffn_w_down, moe_router, moe_w_gate, moe_w_up, moe_w_down,
              norm_final):
    n_past = page_table.shape[1] * cache_fox_k.shape[2]
    pos_p = jnp.arange(x_prompt.shape[1], dtype=jnp.int32)
    pos_s = n_past + jnp.arange(x_sample.shape[1], dtype=jnp.int32)
    xp, xs = x_prompt, x_sample
    st_p, st_s = [], []
    for l in range(DEPTH):
        lw = (norm_mix[l], w_in[l], b_forget[l], w_pool[l], pool_scale[l], w_out[l],
              norm_x[l], w_xq[l], w_xo[l], norm_ffn[l])
        i = l // 2
        if l % 2 == 0:
            ffn = functools.partial(swiglu, w_gate=ffn_w_gate[i], w_up=ffn_w_up[i], w_down=ffn_w_down[i])
        else:
            ffn = functools.partial(moe_swiglu, w_router=moe_router[i], w_gate=moe_w_gate[i],
                                    w_up=moe_w_up[i], w_down=moe_w_down[i])
        mk_p, mv_p = memory_kv(mem_prompt, norm_mem[l], w_xkv[l])
        xp, sp = decoder_layer(xp, pos_p, None, mk_p, mv_p, lw, ffn)
        st_p.append(sp + (mk_p, mv_p))
        past = (gather_pages(cache_fox_k, l, page_table), gather_pages(cache_fox_v, l, page_table),
                gather_pages(cache_fox_logf, l, page_table), gather_pages(cache_dsa_k, l, page_table),
                gather_pages(cache_dsa_v, l, page_table), gather_pages(cache_dsa_kidx, l, page_table),
                state_pool[l])
        xs, ss = decoder_layer(xs, pos_s, past, cache_mem_k[l], cache_mem_v[l], lw, ffn)
        st_s.append(ss)
    y_prompt = rmsnorm(xp, norm_final)
    y_sample = rmsnorm(xs, norm_final)
    fox_k_p = stack_layers(st_p, 0)
    fox_v_p = stack_layers(st_p, 1)
    fox_logf_p = stack_layers(st_p, 2)
    dsa_k_p = stack_layers(st_p, 3)
    dsa_v_p = stack_layers(st_p, 4)
    dsa_kidx_p = stack_layers(st_p, 5)
    pool_p = stack_layers(st_p, 6)
    mem_k_p = stack_layers(st_p, 7)
    mem_v_p = stack_layers(st_p, 8)
    fox_k_s = stack_layers(st_s, 0)
    fox_v_s = stack_layers(st_s, 1)
    fox_logf_s = stack_layers(st_s, 2)
    dsa_k_s = stack_layers(st_s, 3)
    dsa_v_s = stack_layers(st_s, 4)
    dsa_kidx_s = stack_layers(st_s, 5)
    pool_s = stack_layers(st_s, 6)
    return (y_prompt, y_sample, fox_k_p, fox_v_p, fox_logf_p, dsa_k_p, dsa_v_p, dsa_kidx_p, pool_p, mem_k_p, mem_v_p,
            fox_k_s, fox_v_s, fox_logf_s, dsa_k_s, dsa_v_s, dsa_kidx_s, pool_s)
```

```python
import functools

import jax
import jax.numpy as jnp
import numpy as np
from jax import lax
from jax.experimental import pallas as pl
from jax.experimental.pallas import tpu as pltpu

F32 = jnp.float32
BF16 = jnp.bfloat16
I32 = jnp.int32

LANES = 128
HEAD_DIM = 64
FOX_HEADS = 8
DSA_HEADS = 4
IDX_HEADS = 8
IDX_DIM = 32
X_HEADS = 4
POOL_WINDOWS = (2, 4, 8, 16)
POOL_CH = 64
POOL_BUF = POOL_WINDOWS[-1] - 1
TOPK_MAX = 256
ROPE_THETA = 500000.0
N_EXPERTS = 8
EPS = 1e-6
NEG = -1e30
INT_MIN = -(2 ** 31)
VMEM_LIMIT = 56 * 1024 * 1024

_PW = (0, 512, 1024, 1536, 1792, 2048, 2304, 2560, 2688, 2944)
_MISC_KI, _MISC_LF, _MISC_WI = 0, 32, 40


def _cparams(sem):
    return pltpu.CompilerParams(dimension_semantics=sem, vmem_limit_bytes=VMEM_LIMIT)


def _dot(a, b):
    return jnp.dot(a, b, preferred_element_type=F32)


def _dot_nt(a, b):
    return lax.dot_general(a, b, (((1,), (1,)), ((), ())), preferred_element_type=F32)


def _rms(x, g):
    return x * lax.rsqrt(jnp.mean(x * x, axis=-1, keepdims=True) + EPS) * g


def _head_mask(shape, head, width, dtype):
    lane = lax.broadcasted_iota(I32, shape, len(shape) - 1)
    return jnp.where((lane >= head * width) & (lane < (head + 1) * width), 1.0, 0.0).astype(dtype)


def _rope(z, tab, half):
    n = z.shape[1]
    rep = n // LANES

    def wide(a):
        return a if rep == 1 else jnp.concatenate([a] * rep, axis=1)

    c = wide(tab[:, 0:LANES])
    sa = wide(tab[:, LANES:2 * LANES])
    sb = wide(tab[:, 2 * LANES:3 * LANES])
    return z * c + pltpu.roll(z, n - half, 1) * sa + pltpu.roll(z, half, 1) * sb


def _proj_kernel(x_ref, g_ref, w_ref, td_ref, ti_ref, tm_ref, bf_ref,
                 ka_ref, va_ref, kb_ref, vb_ref, misc_ref, u_ref,
                 qa16, ka16, va16, qb16, kb16, vb16, qi16):
    h = _rms(x_ref[...], g_ref[...]).astype(BF16)

    def mm(i):
        return _dot(h, w_ref[:, _PW[i]:_PW[i + 1]])

    scale = HEAD_DIM ** -0.5
    qa16[...] = (mm(0) * scale).astype(BF16)
    z = mm(1)
    ka_ref[...] = z
    ka16[...] = z.astype(BF16)
    z = mm(2)
    va_ref[...] = z
    va16[...] = z.astype(BF16)
    td = td_ref[...]
    qb16[...] = (_rope(mm(3), td, 8) * scale).astype(BF16)
    z = _rope(mm(4), td, 8)
    kb_ref[...] = z
    kb16[...] = z.astype(BF16)
    z = mm(5)
    vb_ref[...] = z
    vb16[...] = z.astype(BF16)
    qi16[...] = _rope(mm(6), ti_ref[...], 4).astype(BF16)
    z = mm(7)
    zr = _rope(z, tm_ref[...], 4)
    a = z + bf_ref[...]
    logsig = jnp.minimum(a, 0.0) - jnp.log(1.0 + jnp.exp(-jnp.abs(a)))
    lane = lax.broadcasted_iota(I32, z.shape, 1)
    misc_ref[...] = jnp.where((lane >= _MISC_LF) & (lane < _MISC_WI), logsig, zr)
    u_ref[...] = mm(8)


def _proj(x, g, wp, td, ti, tmi, bfp, tm, n_pos_tiles):
    n, d = x.shape
    grid = (n // tm,)
    row = lambda i: (i, 0)
    const = lambda i: (0, 0)
    tab = lambda i: (i % n_pos_tiles, 0)
    f32_w = (512, 512, 256, 256, 128, 256)
    b16_w = (512, 512, 512, 256, 256, 256, 256)
    out_shape = [jax.ShapeDtypeStruct((n, w), F32) for w in f32_w] + \
                [jax.ShapeDtypeStruct((n, w), BF16) for w in b16_w]
    out_specs = [pl.BlockSpec((tm, w), row) for w in f32_w + b16_w]
    return pl.pallas_call(
        _proj_kernel,
        grid=grid,
        in_specs=[pl.BlockSpec((tm, d), row), pl.BlockSpec((1, d), const),
                  pl.BlockSpec(wp.shape, const),
                  pl.BlockSpec((tm, 3 * LANES), tab), pl.BlockSpec((tm, 3 * LANES), tab),
                  pl.BlockSpec((tm, 3 * LANES), tab), pl.BlockSpec((1, LANES), const)],
        out_specs=out_specs,
        out_shape=out_shape,
        compiler_params=_cparams(("arbitrary",)),
        name="proj_in",
    )(x, g, wp, td, ti, tmi, bfp)


def _rope_table(pos, head, rot, active):
    half = rot // 2
    inv = jnp.float32(ROPE_THETA) ** (-jnp.arange(half, dtype=jnp.float32) / half)
    ang = pos.astype(jnp.float32)[:, None] * inv[None, :]
    cos, sin = jnp.cos(ang), jnp.sin(ang)
    lane = np.arange(LANES)
    jj = lane % head
    idx = jj % half
    on = lane < active
    c = jnp.where((jj < rot) & on, cos[:, idx], 1.0)
    sa = jnp.where((jj < half) & on, -sin[:, idx], 0.0)
    sb = jnp.where((jj >= half) & (jj < rot) & on, sin[:, idx], 0.0)
    return jnp.concatenate([c, sa, sb], axis=1).astype(F32)


def _nmm_kernel(x_ref, g_ref, w_ref, o_ref, o16_ref):
    z = _dot(_rms(x_ref[...], g_ref[...]).astype(BF16), w_ref[...])
    o_ref[...] = z
    o16_ref[...] = z.astype(BF16)


def _norm_matmul(x, g, w16, tm):
    n, d = x.shape
    m = w16.shape[1]
    return pl.pallas_call(
        _nmm_kernel,
        grid=(n // tm,),
        in_specs=[pl.BlockSpec((tm, d), lambda i: (i, 0)), pl.BlockSpec((1, d), lambda i: (0, 0)),
                  pl.BlockSpec((d, m), lambda i: (0, 0))],
        out_specs=[pl.BlockSpec((tm, m), lambda i: (i, 0)), pl.BlockSpec((tm, m), lambda i: (i, 0))],
        out_shape=[jax.ShapeDtypeStruct((n, m), F32), jax.ShapeDtypeStruct((n, m), BF16)],
        compiler_params=_cparams(("arbitrary",)),
        name="norm_matmul",
    )(x, g, w16)


def _fox_kernel(q_ref, k_ref, v_ref, cfq_ref, cft_ref, o_ref, *, tq, tk):
    hp = pl.program_id(1)
    qt = pl.program_id(2)
    q = q_ref[...]
    cfq = cfq_ref[0]
    lane8 = lax.broadcasted_iota(I32, cfq.shape, 1)
    row = qt * tq + lax.broadcasted_iota(I32, (tq, tk), 0)
    coli = lax.broadcasted_iota(I32, (tq, tk), 1)
    outs = []
    for i in range(2):
        h = 2 * hp + i
        qm = q * _head_mask((1, LANES), i, HEAD_DIM, BF16)
        cq = jnp.sum(jnp.where(lane8 == h, cfq, 0.0), axis=1, keepdims=True)

        def step(kt, carry, masked, qm=qm, cq=cq, h=h):
            m, l, acc = carry
            k0 = pl.multiple_of(kt * tk, tk)
            s = _dot_nt(qm, k_ref[pl.ds(k0, tk), :])
            ck = cft_ref[0, pl.ds(h, 1), pl.ds(kt, 1), :].reshape(1, tk)
            s = s + cq - ck
            if masked:
                s = jnp.where(k0 + coli <= row, s, NEG)
            m_new = jnp.maximum(m, jnp.max(s, axis=1, keepdims=True))
            alpha = jnp.exp(m - m_new)
            p = jnp.exp(s - m_new)
            l = alpha * l + jnp.sum(p, axis=1, keepdims=True)
            acc = alpha * acc + _dot(p.astype(BF16), v_ref[pl.ds(k0, tk), :])
            return m_new, l, acc

        nfull = qt * (tq // tk)
        carry = (jnp.full((tq, 1), NEG, F32), jnp.zeros((tq, 1), F32), jnp.zeros((tq, LANES), F32))
        carry = lax.fori_loop(0, nfull, functools.partial(step, masked=False), carry)
        for d in range(tq // tk):
            carry = step(nfull + d, carry, True)
        outs.append(carry[2] / carry[1])
    lane = lax.broadcasted_iota(I32, (tq, LANES), 1)
    o_ref[...] = jnp.where(lane < HEAD_DIM, outs[0], outs[1]).astype(BF16)


def _fox_prompt(qa16, ka16, va16, cf, cft4, b, t, tq, tk):
    nq = t // tq
    n = b * t
    return pl.pallas_call(
        functools.partial(_fox_kernel, tq=tq, tk=tk),
        grid=(b, FOX_HEADS // 2, nq),
        in_specs=[pl.BlockSpec((tq, LANES), lambda bi, hp, qi: (bi * nq + qi, hp)),
                  pl.BlockSpec((t, LANES), lambda bi, hp, qi: (bi, hp)),
                  pl.BlockSpec((t, LANES), lambda bi, hp, qi: (bi, hp)),
                  pl.BlockSpec((1, tq, FOX_HEADS), lambda bi, hp, qi: (bi, qi, 0)),
                  pl.BlockSpec((1, FOX_HEADS, t // tk, tk), lambda bi, hp, qi: (bi, 0, 0, 0))],
        out_specs=pl.BlockSpec((tq, LANES), lambda bi, hp, qi: (bi * nq + qi, hp)),
        out_shape=jax.ShapeDtypeStruct((n, FOX_HEADS * HEAD_DIM), BF16),
        compiler_params=_cparams(("arbitrary", "arbitrary", "arbitrary")),
        name="fox_prompt",
    )(qa16, ka16, va16, cf, cft4)


def _sort_key(score, valid):
    bits = lax.bitcast_convert_type(score, I32)
    key = bits ^ (lax.shift_right_arithmetic(bits, 31) & 0x7FFFFFFF)
    key = jnp.where(score == 0.0, 0, key)
    return jnp.where(valid, key, INT_MIN)


def _select_threshold(keys_ref, nch, rows, width, ksel):
    nsl = width // LANES
    kf = float(ksel)

    def count_ge(cand):
        def body(c, acc):
            m = jnp.where(keys_ref[c] >= cand, 1.0, 0.0)
            for s in range(nsl):
                acc = acc + m[:, s * LANES:(s + 1) * LANES]
            return acc

        acc = lax.fori_loop(0, nch, body, jnp.zeros((rows, LANES), F32))
        return jnp.sum(acc, axis=1, keepdims=True)

    tau = jnp.where(count_ge(jnp.zeros((rows, 1), I32)) >= kf, 0, INT_MIN).astype(I32)

    def bit_body(bi, tau):
        cand = tau | lax.shift_left(jnp.int32(1), 30 - bi)
        return jnp.where(count_ge(cand) >= kf, cand, tau)

    tau = lax.fori_loop(0, 31, bit_body, tau)
    tie = (tau > INT_MIN) & (count_ge(tau) > kf)

    @pl.when(jnp.max(jnp.where(tie, 1.0, 0.0)) > 0.0)
    def _():
        need = kf - count_ge(tau + 1)
        r_ = lax.broadcasted_iota(I32, (LANES, LANES), 0)
        c_ = lax.broadcasted_iota(I32, (LANES, LANES), 1)
        tri = jnp.where(r_ < c_, 1.0, 0.0).astype(BF16)

        def body(c, carry):
            for s in range(nsl):
                kk = keys_ref[c, :, s * LANES:(s + 1) * LANES]
                eq = kk == tau
                eqf = jnp.where(eq, 1.0, 0.0)
                pre = _dot(eqf.astype(BF16), tri) + carry
                drop = eq & (pre >= need) & tie
                keys_ref[c, :, s * LANES:(s + 1) * LANES] = jnp.where(drop, INT_MIN, kk)
                carry = carry + jnp.sum(eqf, axis=1, keepdims=True)
            return carry

        lax.fori_loop(0, nch, body, jnp.zeros((rows, 1), F32))

    return jnp.maximum(tau, INT_MIN + 1)


def _dsa_kernel(qi_ref, wi_ref, kir_ref, q_ref, k_ref, v_ref, o_ref, keys_ref, *, tq, w, n_sel):
    qt = pl.program_id(1)
    nch = ((qt + 1) * tq + w - 1) // w
    qi = qi_ref[...]
    wi = wi_ref[...]
    rowpos = qt * tq + lax.broadcasted_iota(I32, (tq, w), 0)
    coli = lax.broadcasted_iota(I32, (tq, w), 1)
    qms = [qi * _head_mask((1, IDX_HEADS * IDX_DIM), j, IDX_DIM, BF16) for j in range(IDX_HEADS)]
    wcols = [wi[:, j:j + 1] for j in range(IDX_HEADS)]

    def score_chunk(c, carry):
        k0 = pl.multiple_of(c * w, w)
        kk = kir_ref[pl.ds(k0, w), :]
        sc = jnp.zeros((tq, w), F32)
        for j in range(IDX_HEADS):
            sc = sc + wcols[j] * jnp.maximum(_dot_nt(qms[j], kk), 0.0)
        keys_ref[c] = _sort_key(sc, k0 + coli <= rowpos)
        return carry

    lax.fori_loop(0, nch, score_chunk, 0)
    tau = _select_threshold(keys_ref, nch, tq, w, n_sel)

    q = q_ref[...]
    nl = DSA_HEADS * HEAD_DIM
    out = jnp.zeros((tq, nl), F32)
    for h in range(DSA_HEADS):
        hm = _head_mask((1, nl), h, HEAD_DIM, F32)
        qm = q * hm.astype(BF16)

        def step(c, carry, qm=qm):
            m, l, acc = carry
            k0 = pl.multiple_of(c * w, w)
            s = _dot_nt(qm, k_ref[pl.ds(k0, w), :])
            s = jnp.where(keys_ref[c] >= tau, s, NEG)
            m_new = jnp.maximum(m, jnp.max(s, axis=1, keepdims=True))
            alpha = jnp.exp(m - m_new)
            p = jnp.exp(s - m_new)
            l = alpha * l + jnp.sum(p, axis=1, keepdims=True)
            acc = alpha * acc + _dot(p.astype(BF16), v_ref[pl.ds(k0, w), :])
            return m_new, l, acc

        carry = (jnp.full((tq, 1), NEG, F32), jnp.zeros((tq, 1), F32), jnp.zeros((tq, nl), F32))
        m, l, acc = lax.fori_loop(0, nch, step, carry)
        out = out + (acc / l) * hm
    o_ref[...] = out.astype(BF16)


def _dsa_prompt(qi16, wi, kir16, qb16, kb16, vb16, b, t, tq, w):
    nq = t // tq
    n = b * t
    nl = DSA_HEADS * HEAD_DIM
    n_sel = min(TOPK_MAX, t // 4)
    qrow = lambda bi, qi: (bi * nq + qi, 0)
    full = lambda bi, qi: (bi, 0)
    return pl.pallas_call(
        functools.partial(_dsa_kernel, tq=tq, w=w, n_sel=n_sel),
        grid=(b, nq),
        in_specs=[pl.BlockSpec((tq, nl), qrow), pl.BlockSpec((tq, IDX_HEADS), qrow),
                  pl.BlockSpec((t, nl), full), pl.BlockSpec((tq, nl), qrow),
                  pl.BlockSpec((t, nl), full), pl.BlockSpec((t, nl), full)],
        out_specs=pl.BlockSpec((tq, nl), qrow),
        out_shape=jax.ShapeDtypeStruct((n, nl), BF16),
        scratch_shapes=[pltpu.VMEM((t // w, tq, w), I32)],
        compiler_params=_cparams(("arbitrary", "arbitrary")),
        name="dsa_prompt",
    )(qi16, wi, kir16, qb16, kb16, vb16)


def _pool_mix(win_sums, x, cnts, wblk, scale):
    lane = lax.broadcasted_iota(I32, x.shape, 1)
    mean = jnp.zeros_like(x)
    for g, wdw in enumerate(POOL_WINDOWS):
        mg = win_sums[wdw] / cnts[g]
        mean = jnp.where((lane >= g * POOL_CH) & (lane < (g + 1) * POOL_CH), mg, mean)
    return (_dot((mean - x).astype(BF16), wblk) * scale).astype(BF16)


def _pool_kernel(u_ref, wblk_ref, scale_ref, o_ref, ext_ref, *, tt):
    t = pl.program_id(1)
    hist = POOL_WINDOWS[-1]

    @pl.when(t == 0)
    def _():
        ext_ref[0:hist, :] = jnp.zeros((hist, ext_ref.shape[1]), F32)

    x = u_ref[...]
    ext_ref[hist:hist + tt, :] = x
    acc = x
    sums = {}
    for s in range(1, hist):
        acc = acc + ext_ref[hist - s:hist - s + tt, :]
        if s + 1 in POOL_WINDOWS:
            sums[s + 1] = acc
    pos = t * tt + lax.broadcasted_iota(I32, (tt, 1), 0)
    cnts = [jnp.minimum(pos + 1, wdw).astype(F32) for wdw in POOL_WINDOWS]
    o_ref[...] = _pool_mix(sums, x, cnts, wblk_ref[...], scale_ref[...])
    ext_ref[0:hist, :] = ext_ref[tt:tt + hist, :]


def _pool_prompt(u, wblk16, scale, b, t, tt):
    n, c = u.shape
    nt = t // tt
    return pl.pallas_call(
        functools.partial(_pool_kernel, tt=tt),
        grid=(b, nt),
        in_specs=[pl.BlockSpec((tt, c), lambda bi, ti: (bi * nt + ti, 0)),
                  pl.BlockSpec((c, c), lambda bi, ti: (0, 0)),
                  pl.BlockSpec((1, c), lambda bi, ti: (0, 0))],
        out_specs=pl.BlockSpec((tt, c), lambda bi, ti: (bi * nt + ti, 0)),
        out_shape=jax.ShapeDtypeStruct((n, c), BF16),
        scratch_shapes=[pltpu.VMEM((tt + POOL_WINDOWS[-1], c), F32)],
        compiler_params=_cparams(("arbitrary", "arbitrary")),
        name="pool_prompt",
    )(u, wblk16, scale)


def _pool_s_kernel(uu_ref, wblk_ref, scale_ref, o_ref):
    c = o_ref.shape[1]
    hist = POOL_WINDOWS[-1]
    x = uu_ref[:, (hist - 1) * c:hist * c]
    acc = x
    sums = {}
    for s in range(1, hist):
        acc = acc + uu_ref[:, (hist - 1 - s) * c:(hist - s) * c]
        if s + 1 in POOL_WINDOWS:
            sums[s + 1] = acc
    cnts = [float(wdw) for wdw in POOL_WINDOWS]
    o_ref[...] = _pool_mix(sums, x, cnts, wblk_ref[...], scale_ref[...])


def _pool_sample(uu2, wblk16, scale):
    n = uu2.shape[0]
    c = wblk16.shape[0]
    return pl.pallas_call(
        _pool_s_kernel,
        out_shape=jax.ShapeDtypeStruct((n, c), BF16),
        name="pool_sample",
    )(uu2, wblk16, scale)


def _mix_out(x, oa, ob, oc, wo_ref):
    na, nb = oa.shape[1], ob.shape[1]
    return (x + _dot(oa, wo_ref[0:na, :]) + _dot(ob, wo_ref[na:na + nb, :])
            + _dot(oc, wo_ref[na + nb:, :]))


def _mixx_kernel(x_ref, oa_ref, ob_ref, oc_ref, wo_ref, gx_ref, wxq_ref, mk_ref, mv_ref, wxo_ref,
                 o_ref):
    x1 = _mix_out(x_ref[...], oa_ref[...], ob_ref[...], oc_ref[...], wo_ref)
    h = _rms(x1, gx_ref[...]).astype(BF16)
    q = (_dot(h, wxq_ref[...]) * HEAD_DIM ** -0.5).astype(BF16)
    mk = mk_ref[0]
    mv = mv_ref[0]
    nl = X_HEADS * HEAD_DIM
    o = jnp.zeros(q.shape, F32)
    for hh in range(X_HEADS):
        hm = _head_mask((1, nl), hh, HEAD_DIM, F32)
        s = _dot_nt(q * hm.astype(BF16), mk)
        p = jnp.exp(s - jnp.max(s, axis=1, keepdims=True))
        l = jnp.sum(p, axis=1, keepdims=True)
        o = o + (_dot(p.astype(BF16), mv) / l) * hm
    o_ref[...] = x1 + _dot(o.astype(BF16), wxo_ref[...])


def _mixx_prompt(x, oa, ob, oc, wo16, gx, wxq16, mk16, mv16, wxo16, t, tm):
    n, d = x.shape
    per_b = t // tm
    row = lambda i: (i, 0)
    const = lambda i: (0, 0)
    mem = lambda i: (i // per_b, 0, 0)
    return pl.pallas_call(
        _mixx_kernel,
        grid=(n // tm,),
        in_specs=[pl.BlockSpec((tm, d), row), pl.BlockSpec((tm, oa.shape[1]), row),
                  pl.BlockSpec((tm, ob.shape[1]), row), pl.BlockSpec((tm, oc.shape[1]), row),
                  pl.BlockSpec(wo16.shape, const), pl.BlockSpec((1, d), const),
                  pl.BlockSpec(wxq16.shape, const),
                  pl.BlockSpec((1,) + mk16.shape[1:], mem), pl.BlockSpec((1,) + mv16.shape[1:], mem),
                  pl.BlockSpec(wxo16.shape, const)],
        out_specs=pl.BlockSpec((tm, d), row),
        out_shape=jax.ShapeDtypeStruct((n, d), F32),
        compiler_params=_cparams(("arbitrary",)),
        name="mix_xattn_prompt",
    )(x, oa, ob, oc, wo16, gx, wxq16, mk16, mv16, wxo16)


def _rows_by_head(v, nrows, width):
    lanes = v.shape[1]
    r = lax.broadcasted_iota(I32, (nrows, lanes), 0)
    lane = lax.broadcasted_iota(I32, (nrows, lanes), 1)
    keep = (lane >= r * width) & (lane < (r + 1) * width)
    return jnp.where(keep, jnp.broadcast_to(v, (nrows, lanes)), 0.0), keep


def _mixx_s_kernel(x_ref, oa_ref, ob_ref, oc_ref, wo_ref, gx_ref, wxq_ref, mk_ref, mv_ref, wxo_ref,
                   o_ref, att_ref, *, sb):
    x1 = _mix_out(x_ref[...], oa_ref[...], ob_ref[...], oc_ref[...], wo_ref)
    h = _rms(x1, gx_ref[...]).astype(BF16)
    q = _dot(h, wxq_ref[...]) * HEAD_DIM ** -0.5
    for b in range(sb):
        qrows, keep = _rows_by_head(q[b:b + 1, :], 8, HEAD_DIM)
        s = _dot_nt(qrows.astype(BF16), mk_ref[0, b].astype(BF16))
        p = jnp.exp(s - jnp.max(s, axis=1, keepdims=True))
        l = jnp.sum(p, axis=1, keepdims=True)
        ob_ = _dot(p.astype(BF16), mv_ref[0, b].astype(BF16)) / l
        att_ref[b:b + 1, :] = jnp.sum(jnp.where(keep, ob_, 0.0), axis=0, keepdims=True)
    o_ref[...] = x1 + _dot(att_ref[...].astype(BF16), wxo_ref[...])


def _mixx_sample(x, oa, ob, oc, wo16, gx, wxq16, memk, memv, layer, wxo16, sb):
    n, d = x.shape
    nl = X_HEADS * HEAD_DIM
    row = lambda i: (i, 0)
    const = lambda i: (0, 0)
    mem = lambda i: (layer, i, 0, 0)
    mblk = (1, sb) + memk.shape[2:]
    return pl.pallas_call(
        functools.partial(_mixx_s_kernel, sb=sb),
        grid=(n // sb,),
        in_specs=[pl.BlockSpec((sb, d), row), pl.BlockSpec((sb, oa.shape[1]), row),
                  pl.BlockSpec((sb, ob.shape[1]), row), pl.BlockSpec((sb, oc.shape[1]), row),
                  pl.BlockSpec(wo16.shape, const), pl.BlockSpec((1, d), const),
                  pl.BlockSpec(wxq16.shape, const),
                  pl.BlockSpec(mblk, mem), pl.BlockSpec(mblk, mem),
                  pl.BlockSpec(wxo16.shape, const)],
        out_specs=pl.BlockSpec((sb, d), row),
        out_shape=jax.ShapeDtypeStruct((n, d), F32),
        scratch_shapes=[pltpu.VMEM((sb, nl), F32)],
        compiler_params=_cparams(("arbitrary",)),
        name="mix_xattn_sample",
    )(x, oa, ob, oc, wo16, gx, wxq16, memk, memv, wxo16)


def _ffn_kernel(eid_ref, x_ref, g_ref, wg_ref, wu_ref, wd_ref, o_ref, h_ref, acc_ref, *, dense):
    j = pl.program_id(1)

    @pl.when(j == 0)
    def _():
        if dense:
            h_ref[...] = _rms(x_ref[...], g_ref[...]).astype(BF16)
        else:
            h_ref[...] = x_ref[...]
        acc_ref[...] = jnp.zeros(acc_ref.shape, F32)

    h = h_ref[...]
    a = _dot(h, wg_ref[0])
    b = _dot(h, wu_ref[0])
    t = (a * (1.0 / (1.0 + jnp.exp(-a)))) * b
    acc_ref[...] += _dot(t.astype(BF16), wd_ref[0])

    @pl.when(j == pl.num_programs(1) - 1)
    def _():
        if dense:
            o_ref[...] = x_ref[...] + acc_ref[...]
        else:
            o_ref[...] = acc_ref[...]


def _ffn(eid, x, g, wg16, wu16, wd16, bm, tf, dense):
    n, d = x.shape
    dff = wg16.shape[2]
    grid = (n // bm, dff // tf)
    gs = pltpu.PrefetchScalarGridSpec(
        num_scalar_prefetch=1,
        grid=grid,
        in_specs=[pl.BlockSpec((bm, d), lambda i, j, e: (i, 0)),
                  pl.BlockSpec((1, d), lambda i, j, e: (0, 0)),
                  pl.BlockSpec((1, d, tf), lambda i, j, e: (e[i], 0, j)),
                  pl.BlockSpec((1, d, tf), lambda i, j, e: (e[i], 0, j)),
                  pl.BlockSpec((1, tf, d), lambda i, j, e: (e[i], j, 0))],
        out_specs=pl.BlockSpec((bm, d), lambda i, j, e: (i, 0)),
        scratch_shapes=[pltpu.VMEM((bm, d), BF16), pltpu.VMEM((bm, d), F32)],
    )
    return pl.pallas_call(
        functools.partial(_ffn_kernel, dense=dense),
        grid_spec=gs,
        out_shape=jax.ShapeDtypeStruct((n, d), F32),
        compiler_params=_cparams(("arbitrary", "arbitrary")),
        name="ffn_dense" if dense else "ffn_experts",
    )(eid, x, g, wg16, wu16, wd16)


def _router_kernel(x_ref, g_ref, wr_ref, h_ref, info_ref):
    h = _rms(x_ref[...], g_ref[...]).astype(BF16)
    h_ref[...] = h
    logits = _dot(h, wr_ref[...])
    lane = lax.broadcasted_iota(I32, logits.shape, 1)
    logits = jnp.where(lane < N_EXPERTS, logits, NEG)
    m1 = jnp.max(logits, axis=1, keepdims=True)
    i1 = jnp.min(jnp.where(logits == m1, lane, LANES), axis=1, keepdims=True)
    rest = jnp.where(lane == i1, NEG, logits)
    m2 = jnp.max(rest, axis=1, keepdims=True)
    i2 = jnp.min(jnp.where(rest == m2, lane, LANES), axis=1, keepdims=True)
    e = jnp.exp(m2 - m1)
    g1 = 1.0 / (1.0 + e)
    g2 = e / (1.0 + e)
    info = jnp.where(lane == 0, i1.astype(F32), jnp.where(lane == 1, i2.astype(F32),
                     jnp.where(lane == 2, g1, jnp.where(lane == 3, g2, 0.0))))
    info_ref[...] = info


def _router(x, g, wr16, tm):
    n, d = x.shape
    return pl.pallas_call(
        _router_kernel,
        grid=(n // tm,),
        in_specs=[pl.BlockSpec((tm, d), lambda i: (i, 0)), pl.BlockSpec((1, d), lambda i: (0, 0)),
                  pl.BlockSpec((d, LANES), lambda i: (0, 0))],
        out_specs=[pl.BlockSpec((tm, d), lambda i: (i, 0)), pl.BlockSpec((tm, LANES), lambda i: (i, 0))],
        out_shape=[jax.ShapeDtypeStruct((n, d), BF16), jax.ShapeDtypeStruct((n, LANES), F32)],
        compiler_params=_cparams(("arbitrary",)),
        name="moe_router",
    )(x, g, wr16)


def _moe(x, g, wr16, wg16, wu16, wd16, tm, bm, tf):
    n, d = x.shape
    h16, info = _router(x, g, wr16, tm)
    top_e = info[:, 0:2].astype(I32)
    gates = info[:, 2:4]
    n_as = 2 * n
    flat_e = top_e.reshape(n_as)
    order = jnp.argsort(flat_e)
    se = flat_e[order]
    stok = (order // 2).astype(I32)
    counts = jnp.bincount(flat_e, length=N_EXPERTS)
    padded = (counts + bm - 1) // bm * bm
    pad_end = jnp.cumsum(padded)
    pad_start = pad_end - padded
    grp_start = jnp.cumsum(counts) - counts
    dest_sorted = (pad_start[se] + jnp.arange(n_as, dtype=I32) - grp_start[se]).astype(I32)
    nblk = -(-n_as // bm) + N_EXPERTS
    row_tok = jnp.zeros((nblk * bm,), I32).at[dest_sorted].set(stok)
    blk_e = jnp.minimum(jnp.searchsorted(pad_end, jnp.arange(nblk) * bm, side='right'),
                        N_EXPERTS - 1).astype(I32)
    xs = h16[row_tok]
    ys = _ffn(blk_e, xs, g, wg16, wu16, wd16, bm, tf, dense=False)
    dest = jnp.zeros((n_as,), I32).at[order].set(dest_sorted)
    contrib = ys[dest].reshape(n, 2, d) * gates[:, :, None]
    return x + contrib[:, 0] + contrib[:, 1]


def _rmsout_kernel(x_ref, g_ref, o_ref):
    o_ref[...] = _rms(x_ref[...], g_ref[...])


def _rms_out(x, g, tm):
    n, d = x.shape
    return pl.pallas_call(
        _rmsout_kernel,
        grid=(n // tm,),
        in_specs=[pl.BlockSpec((tm, d), lambda i: (i, 0)), pl.BlockSpec((1, d), lambda i: (0, 0))],
        out_specs=pl.BlockSpec((tm, d), lambda i: (i, 0)),
        out_shape=jax.ShapeDtypeStruct((n, d), F32),
        compiler_params=_cparams(("arbitrary",)),
        name="final_norm",
    )(x, g)


def _dsa_score_s_kernel(pt_ref, qi_ref, wi_ref, kn_ref, *rest, n_pages, page):
    pages = rest[:n_pages]
    o_ref = rest[n_pages]
    qi = qi_ref[0]
    wi = wi_ref[0]
    slabs = []
    for p in range(n_pages):
        r = _dot_nt(qi, pages[p][0, 0].astype(BF16))
        slabs.append(jnp.sum(wi * jnp.maximum(r, 0.0), axis=0, keepdims=True))
    kn = kn_ref[0].astype(BF16).astype(F32)
    rn = jnp.sum(qi.astype(F32) * kn, axis=1, keepdims=True)
    sn = jnp.sum(wi * jnp.maximum(rn, 0.0), axis=0, keepdims=True)
    lane = lax.broadcasted_iota(I32, (1, page), 1)
    slabs.append(jnp.where(lane == 0, sn, NEG))
    o_ref[0] = jnp.concatenate(slabs, axis=1)


def _dsa_score_sample(page_table, qi3, wi3, kinew, cache_kidx, layer):
    nb, n_pages = page_table.shape
    page = cache_kidx.shape[2]
    width = (n_pages + 1) * page
    pspecs = [pl.BlockSpec((1, 1, page, IDX_DIM), functools.partial(
        lambda b, pt, p: (layer, pt[b, p], 0, 0), p=p)) for p in range(n_pages)]
    gs = pltpu.PrefetchScalarGridSpec(
        num_scalar_prefetch=1,
        grid=(nb,),
        in_specs=[pl.BlockSpec((1, IDX_HEADS, IDX_DIM), lambda b, pt: (b, 0, 0)),
                  pl.BlockSpec((1, IDX_HEADS, 1), lambda b, pt: (b, 0, 0)),
                  pl.BlockSpec((1, 1, IDX_DIM), lambda b, pt: (b, 0, 0))] + pspecs,
        out_specs=pl.BlockSpec((1, 1, width), lambda b, pt: (b, 0, 0)),
    )
    return pl.pallas_call(
        functools.partial(_dsa_score_s_kernel, n_pages=n_pages, page=page),
        grid_spec=gs,
        out_shape=jax.ShapeDtypeStruct((nb, 1, width), F32),
        compiler_params=_cparams(("arbitrary",)),
        name="dsa_score_sample",
    )(page_table, qi3, wi3, kinew, *([cache_kidx] * n_pages))


def _dsa_select_s_kernel(s_ref, o_ref, keys_ref, *, n_valid, n_sel):
    nch, rows, width = keys_ref.shape
    coli = lax.broadcasted_iota(I32, (rows, width), 1)
    for c in range(nch):
        sc = s_ref[:, c * width:(c + 1) * width]
        keys_ref[c] = _sort_key(sc, c * width + coli < n_valid)
    tau = _select_threshold(keys_ref, nch, rows, width, n_sel)
    for c in range(nch):
        o_ref[:, c * width:(c + 1) * width] = jnp.where(keys_ref[c] >= tau, 0.0, NEG)


def _dsa_select_sample(scores, n_valid, n_sel):
    rows, width = scores.shape
    return pl.pallas_call(
        functools.partial(_dsa_select_s_kernel, n_valid=n_valid, n_sel=n_sel),
        out_shape=jax.ShapeDtypeStruct((rows, width), F32),
        scratch_shapes=[pltpu.VMEM((width // LANES, rows, LANES), I32)],
        name="dsa_select_sample",
    )(scores)


def _decode_attend(q, knew, vnew, k_pages, v_pages, bias_fn, bias_new, nrows):
    page = k_pages[0].shape[2]
    qrows, keep = _rows_by_head(q.astype(F32), nrows, HEAD_DIM)
    q16 = qrows.astype(BF16)
    s = jnp.concatenate(
        [_dot_nt(q16, kp[0, 0].astype(BF16)) + bias_fn(p) for p, kp in enumerate(k_pages)], axis=1)
    kn = knew.astype(BF16).astype(F32)
    sn = jnp.sum(qrows * kn, axis=1, keepdims=True) + bias_new
    m = jnp.maximum(jnp.max(s, axis=1, keepdims=True), sn)
    pr = jnp.exp(s - m)
    pn = jnp.exp(sn - m)
    l = jnp.sum(pr, axis=1, keepdims=True) + pn
    acc = pn * vnew.astype(BF16).astype(F32)
    for p, vp in enumerate(v_pages):
        acc = acc + _dot(pr[:, p * page:(p + 1) * page].astype(BF16), vp[0, 0].astype(BF16))
    return jnp.sum(jnp.where(keep, acc / l, 0.0), axis=0, keepdims=True)


def _decode_kernel(pt_ref, qa_ref, qb_ref, kan_ref, van_ref, kbn_ref, vbn_ref, fb_ref, dm_ref, *rest,
                   n_pages, page):
    fk = rest[0:n_pages]
    fv = rest[n_pages:2 * n_pages]
    dk = rest[2 * n_pages:3 * n_pages]
    dv = rest[3 * n_pages:4 * n_pages]
    oa_ref, ob_ref = rest[4 * n_pages:]
    oa = _decode_attend(qa_ref[0], kan_ref[0], van_ref[0], fk, fv,
                        lambda p: fb_ref[0, p], 0.0, FOX_HEADS)
    oa_ref[0] = oa.astype(BF16)
    ob = _decode_attend(qb_ref[0], kbn_ref[0], vbn_ref[0], dk, dv,
                        lambda p: dm_ref[0, :, p * page:(p + 1) * page],
                        dm_ref[0, :, n_pages * page:n_pages * page + 1], 8)
    ob_ref[0] = ob.astype(BF16)


def _decode_attention(page_table, qa3, qb3, kan, van, kbn, vbn, fbias, dmask,
                      cache_fk, cache_fv, cache_dk, cache_dv, layer):
    nb, n_pages = page_table.shape
    page = cache_fk.shape[2]
    wa = FOX_HEADS * HEAD_DIM
    wb = DSA_HEADS * HEAD_DIM

    def pspecs(width):
        return [pl.BlockSpec((1, 1, page, width), functools.partial(
            lambda b, pt, p: (layer, pt[b, p], 0, 0), p=p)) for p in range(n_pages)]

    vec = lambda w: pl.BlockSpec((1, 1, w), lambda b, pt: (b, 0, 0))
    gs = pltpu.PrefetchScalarGridSpec(
        num_scalar_prefetch=1,
        grid=(nb,),
        in_specs=[vec(wa), vec(wb), vec(wa), vec(wa), vec(wb), vec(wb),
                  pl.BlockSpec((1, n_pages, FOX_HEADS, page), lambda b, pt: (b, 0, 0, 0)),
                  vec(dmask.shape[2])] + pspecs(wa) + pspecs(wa) + pspecs(wb) + pspecs(wb),
        out_specs=[vec(wa), vec(wb)],
    )
    return pl.pallas_call(
        functools.partial(_decode_kernel, n_pages=n_pages, page=page),
        grid_spec=gs,
        out_shape=[jax.ShapeDtypeStruct((nb, 1, wa), BF16), jax.ShapeDtypeStruct((nb, 1, wb), BF16)],
        compiler_params=_cparams(("arbitrary",)),
        name="decode_attention",
    )(page_table, qa3, qb3, kan, van, kbn, vbn, fbias, dmask,
      *([cache_fk] * n_pages + [cache_fv] * n_pages + [cache_dk] * n_pages + [cache_dv] * n_pages))


def _prep_w_in(w_in):
    fw, dw = FOX_HEADS * HEAD_DIM, DSA_HEADS * HEAD_DIM
    o_fa = 3 * fw
    o_qb = o_fa + FOX_HEADS
    o_qi = o_qb + 3 * dw
    o_ki = o_qi + IDX_HEADS * IDX_DIM
    o_wi = o_ki + IDX_DIM
    o_u = o_wi + IDX_HEADS
    d = w_in.shape[0]
    misc = jnp.concatenate([w_in[:, o_ki:o_wi], w_in[:, o_fa:o_qb], w_in[:, o_wi:o_u],
                            jnp.zeros((d, LANES - IDX_DIM - FOX_HEADS - IDX_HEADS), w_in.dtype)], axis=1)
    return jnp.concatenate([w_in[:, 0:o_fa], w_in[:, o_qb:o_qi], w_in[:, o_qi:o_ki], misc,
                            w_in[:, o_u:]], axis=1).astype(BF16)


def _block_diag(w_pool):
    g, c, _ = w_pool.shape
    out = jnp.zeros((g * c, g * c), w_pool.dtype)
    for i in range(g):
        out = out.at[i * c:(i + 1) * c, i * c:(i + 1) * c].set(w_pool[i])
    return out.astype(BF16)


def kernel(x_prompt, x_sample, cache_fox_k, cache_fox_v, cache_fox_logf, cache_dsa_k, cache_dsa_v,
           cache_dsa_kidx, state_pool, cache_mem_k, cache_mem_v, page_table, mem_prompt,
           norm_mix, w_in, b_forget, w_pool, pool_scale, w_out, norm_x, norm_mem, w_xq, w_xkv, w_xo,
           norm_ffn, ffn_w_gate, ffn_w_up, ffn_w_down, moe_router, moe_w_gate, moe_w_up, moe_w_down,
           norm_final):
    b, t, d = x_prompt.shape
    nb = x_sample.shape[0]
    depth = w_in.shape[0]
    n_pool, page = cache_fox_k.shape[1], cache_fox_k.shape[2]
    n_pages = page_table.shape[1]
    n_past = n_pages * page
    n_mem = mem_prompt.shape[1]
    fw, dw = FOX_HEADS * HEAD_DIM, DSA_HEADS * HEAD_DIM
    xw = X_HEADS * HEAD_DIM
    n = b * t

    tm_proj, tq_fox, tk_fox, tq_dsa, w_dsa, tt_pool, tm_mix = 512, 256, 256, 128, 512, 512, 512
    tm_ffn, tf_dense, tm_moe, bm_moe, tf_moe = 1024, 256, 512, 512, 512

    pos_p = jnp.arange(t, dtype=I32)
    pos_s = jnp.full((nb,), n_past, I32)
    tabs = []
    for pos in (pos_p, pos_s):
        tabs.append((_rope_table(pos, HEAD_DIM, HEAD_DIM // 4, LANES),
                     _rope_table(pos, IDX_DIM, IDX_DIM // 4, LANES),
                     _rope_table(pos, IDX_DIM, IDX_DIM // 4, IDX_DIM)))

    cfk = cache_fox_k.reshape(depth, n_pool, page, fw)
    cfv = cache_fox_v.reshape(depth, n_pool, page, fw)
    cdk = cache_dsa_k.reshape(depth, n_pool, page, dw)
    cdv = cache_dsa_v.reshape(depth, n_pool, page, dw)
    cmk = cache_mem_k.reshape(depth, nb, n_mem, xw)
    cmv = cache_mem_v.reshape(depth, nb, n_mem, xw)
    zero_eid_p = jnp.zeros((n // tm_ffn,), I32)
    zero_eid_s = jnp.zeros((1,), I32)

    xp = x_prompt.reshape(n, d)
    xs = x_sample.reshape(nb, d)
    st_p, st_s = [], []
    for l in range(depth):
        wp = _prep_w_in(w_in[l])
        bfp = jnp.zeros((1, LANES), F32).at[0, _MISC_LF:_MISC_WI].set(b_forget[l].astype(F32))
        g_mix = norm_mix[l].reshape(1, d)
        g_x = norm_x[l].reshape(1, d)
        g_ffn = norm_ffn[l].reshape(1, d)
        wblk = _block_diag(w_pool[l])
        pscale = pool_scale[l].reshape(1, -1).astype(F32)
        wo16 = w_out[l].astype(BF16)
        wxq16 = w_xq[l].astype(BF16)
        wxo16 = w_xo[l].astype(BF16)

        (ka, va, kb, vb, misc, u, qa16, ka16, va16, qb16, kb16, vb16, qi16) = _proj(
            xp, g_mix, wp, *tabs[0], bfp, tm_proj, t // tm_proj)
        lf = misc[:, _MISC_LF:_MISC_WI].reshape(b, t, FOX_HEADS)
        cf = jnp.cumsum(lf, axis=1)
        cft4 = jnp.transpose(cf, (0, 2, 1)).reshape(b, FOX_HEADS, t // tk_fox, tk_fox)
        oa = _fox_prompt(qa16, ka16, va16, cf, cft4, b, t, tq_fox, tk_fox)
        ki = misc[:, _MISC_KI:_MISC_KI + IDX_DIM]
        kir16 = jnp.tile(ki.astype(BF16), (1, IDX_HEADS))
        wi = misc[:, _MISC_WI:_MISC_WI + IDX_HEADS]
        ob = _dsa_prompt(qi16, wi, kir16, qb16, kb16, vb16, b, t, tq_dsa, w_dsa)
        oc = _pool_prompt(u, wblk, pscale, b, t, tt_pool)
        kv, kv16 = _norm_matmul(mem_prompt.reshape(b * n_mem, d), norm_mem[l].reshape(1, d),
                                w_xkv[l].astype(BF16), 256)
        mk16 = kv16[:, :xw].reshape(b, n_mem, xw)
        mv16 = kv16[:, xw:].reshape(b, n_mem, xw)
        xp = _mixx_prompt(xp, oa, ob, oc, wo16, g_x, wxq16, mk16, mv16, wxo16, t, tm_mix)
        st_p.append((ka.reshape(b, t, FOX_HEADS, HEAD_DIM), va.reshape(b, t, FOX_HEADS, HEAD_DIM), lf,
                     kb.reshape(b, t, DSA_HEADS, HEAD_DIM), vb.reshape(b, t, DSA_HEADS, HEAD_DIM),
                     ki.reshape(b, t, IDX_DIM), u.reshape(b, t, -1)[:, -POOL_BUF:],
                     kv[:, :xw].reshape(b, n_mem, X_HEADS, HEAD_DIM),
                     kv[:, xw:].reshape(b, n_mem, X_HEADS, HEAD_DIM)))

        (ka_s, va_s, kb_s, vb_s, misc_s, u_s, qa16s, _, _, qb16s, _, _, qi16s) = _proj(
            xs, g_mix, wp, *tabs[1], bfp, nb, 1)
        lf_s = misc_s[:, _MISC_LF:_MISC_WI]
        ki_s = misc_s[:, _MISC_KI:_MISC_KI + IDX_DIM]
        wi_s = misc_s[:, _MISC_WI:_MISC_WI + IDX_HEADS]
        lf_past = cache_fox_logf[l][page_table].reshape(nb, n_past, FOX_HEADS).astype(F32)
        csum = jnp.cumsum(lf_past, axis=1)
        fbias = lf_s[:, None, :] + (csum[:, -1:, :] - csum)
        fbias = jnp.transpose(fbias.reshape(nb, n_pages, page, FOX_HEADS), (0, 1, 3, 2))
        scores = _dsa_score_sample(page_table, qi16s.reshape(nb, IDX_HEADS, IDX_DIM),
                                   wi_s.reshape(nb, IDX_HEADS, 1), ki_s.reshape(nb, 1, IDX_DIM),
                                   cache_dsa_kidx, l)
        dmask = _dsa_select_sample(scores.reshape(nb, -1), n_past + 1,
                                   min(TOPK_MAX, (n_past + 1) // 4))
        oa_s, ob_s = _decode_attention(
            page_table, qa16s.reshape(nb, 1, fw), qb16s.reshape(nb, 1, dw),
            ka_s.reshape(nb, 1, fw), va_s.reshape(nb, 1, fw), kb_s.reshape(nb, 1, dw),
            vb_s.reshape(nb, 1, dw), fbias, dmask.reshape(nb, 1, -1), cfk, cfv, cdk, cdv, l)
        uu = jnp.concatenate([state_pool[l], u_s[:, None, :]], axis=1)
        oc_s = _pool_sample(uu.reshape(nb, -1), wblk, pscale)
        xs = _mixx_sample(xs, oa_s.reshape(nb, fw), ob_s.reshape(nb, dw), oc_s, wo16, g_x, wxq16,
                          cmk, cmv, l, wxo16, 8)
        st_s.append((ka_s.reshape(nb, 1, FOX_HEADS, HEAD_DIM), va_s.reshape(nb, 1, FOX_HEADS, HEAD_DIM),
                     lf_s.reshape(nb, 1, FOX_HEADS), kb_s.reshape(nb, 1, DSA_HEADS, HEAD_DIM),
                     vb_s.reshape(nb, 1, DSA_HEADS, HEAD_DIM), ki_s.reshape(nb, 1, IDX_DIM),
                     uu[:, -POOL_BUF:]))

        i = l // 2
        if l % 2 == 0:
            wg16 = ffn_w_gate[i:i + 1].astype(BF16)
            wu16 = ffn_w_up[i:i + 1].astype(BF16)
            wd16 = ffn_w_down[i:i + 1].astype(BF16)
            xp = _ffn(zero_eid_p, xp, g_ffn, wg16, wu16, wd16, tm_ffn, tf_dense, dense=True)
            xs = _ffn(zero_eid_s, xs, g_ffn, wg16, wu16, wd16, nb, tf_dense, dense=True)
        else:
            wr16 = jnp.zeros((d, LANES), BF16).at[:, :N_EXPERTS].set(moe_router[i].astype(BF16))
            wg16 = moe_w_gate[i].astype(BF16)
            wu16 = moe_w_up[i].astype(BF16)
            wd16 = moe_w_down[i].astype(BF16)
            xp = _moe(xp, g_ffn, wr16, wg16, wu16, wd16, tm_moe, bm_moe, tf_moe)
            xs = _moe(xs, g_ffn, wr16, wg16, wu16, wd16, nb, 128, tf_moe)

    gf = norm_final.reshape(1, d)
    y_prompt = _rms_out(xp, gf, 1024).reshape(b, t, d)
    y_sample = _rms_out(xs, gf, nb).reshape(nb, 1, d)
    stack = lambda states, j: jnp.stack([s[j] for s in states], axis=0)
    return (y_prompt, y_sample) + tuple(stack(st_p, j) for j in range(9)) + \
        tuple(stack(st_s, j) for j in range(7))
```

```python
import functools

import jax
import jax.numpy as jnp
import numpy as np
from jax import lax
from jax.experimental import pallas as pl
from jax.experimental.pallas import tpu as pltpu

F32 = jnp.float32
BF16 = jnp.bfloat16
I32 = jnp.int32

LANES = 128
HEAD_DIM = 64
FOX_HEADS = 8
DSA_HEADS = 4
IDX_HEADS = 8
IDX_DIM = 32
X_HEADS = 4
POOL_WINDOWS = (2, 4, 8, 16)
POOL_CH = 64
POOL_BUF = POOL_WINDOWS[-1] - 1
TOPK_MAX = 256
ROPE_THETA = 500000.0
N_EXPERTS = 8
EPS = 1e-6
NEG = -1e30
INT_MIN = -(2 ** 31)
VMEM_LIMIT = 56 * 1024 * 1024

_PW = (0, 512, 1024, 1536, 1792, 2048, 2304, 2560, 2688, 2944)
_MISC_KI, _MISC_LF, _MISC_WI = 0, 32, 40


def _cparams(sem):
    return pltpu.CompilerParams(dimension_semantics=sem, vmem_limit_bytes=VMEM_LIMIT)


def _dot(a, b):
    return jnp.dot(a, b, preferred_element_type=F32)


def _dot_nt(a, b):
    return lax.dot_general(a, b, (((1,), (1,)), ((), ())), preferred_element_type=F32)


def _rms(x, g):
    return x * lax.rsqrt(jnp.mean(x * x, axis=-1, keepdims=True) + EPS) * g


def _head_mask(shape, head, width, dtype):
    lane = lax.broadcasted_iota(I32, shape, len(shape) - 1)
    return jnp.where((lane >= head * width) & (lane < (head + 1) * width), 1.0, 0.0).astype(dtype)


def _rope(z, tab, half):
    n = z.shape[1]
    rep = n // LANES

    def wide(a):
        return a if rep == 1 else jnp.concatenate([a] * rep, axis=1)

    c = wide(tab[:, 0:LANES])
    sa = wide(tab[:, LANES:2 * LANES])
    sb = wide(tab[:, 2 * LANES:3 * LANES])
    return z * c + pltpu.roll(z, n - half, 1) * sa + pltpu.roll(z, half, 1) * sb


def _proj_kernel(x_ref, g_ref, w_ref, td_ref, ti_ref, tm_ref, bf_ref,
                 ka_ref, va_ref, kb_ref, vb_ref, misc_ref, u_ref,
                 qa16, ka16, va16, qb16, kb16, vb16, qi16):
    h = _rms(x_ref[...], g_ref[...]).astype(BF16)

    def mm(i):
        return _dot(h, w_ref[:, _PW[i]:_PW[i + 1]])

    scale = HEAD_DIM ** -0.5
    qa16[...] = (mm(0) * scale).astype(BF16)
    z = mm(1)
    ka_ref[...] = z
    ka16[...] = z.astype(BF16)
    z = mm(2)
    va_ref[...] = z
    va16[...] = z.astype(BF16)
    td = td_ref[...]
    qb16[...] = (_rope(mm(3), td, 8) * scale).astype(BF16)
    z = _rope(mm(4), td, 8)
    kb_ref[...] = z
    kb16[...] = z.astype(BF16)
    z = mm(5)
    vb_ref[...] = z
    vb16[...] = z.astype(BF16)
    qi16[...] = _rope(mm(6), ti_ref[...], 4).astype(BF16)
    z = mm(7)
    zr = _rope(z, tm_ref[...], 4)
    a = z + bf_ref[...]
    logsig = jnp.minimum(a, 0.0) - jnp.log(1.0 + jnp.exp(-jnp.abs(a)))
    lane = lax.broadcasted_iota(I32, z.shape, 1)
    misc_ref[...] = jnp.where((lane >= _MISC_LF) & (lane < _MISC_WI), logsig, zr)
    u_ref[...] = mm(8)


def _proj(x, g, wp, td, ti, tmi, bfp, tm, n_pos_tiles):
    n, d = x.shape
    grid = (n // tm,)
    row = lambda i: (i, 0)
    const = lambda i: (0, 0)
    tab = lambda i: (i % n_pos_tiles, 0)
    f32_w = (512, 512, 256, 256, 128, 256)
    b16_w = (512, 512, 512, 256, 256, 256, 256)
    out_shape = [jax.ShapeDtypeStruct((n, w), F32) for w in f32_w] + \
                [jax.ShapeDtypeStruct((n, w), BF16) for w in b16_w]
    out_specs = [pl.BlockSpec((tm, w), row) for w in f32_w + b16_w]
    return pl.pallas_call(
        _proj_kernel,
        grid=grid,
        in_specs=[pl.BlockSpec((tm, d), row), pl.BlockSpec((1, d), const),
                  pl.BlockSpec(wp.shape, const),
                  pl.BlockSpec((tm, 3 * LANES), tab), pl.BlockSpec((tm, 3 * LANES), tab),
                  pl.BlockSpec((tm, 3 * LANES), tab), pl.BlockSpec((1, LANES), const)],
        out_specs=out_specs,
        out_shape=out_shape,
        compiler_params=_cparams(("arbitrary",)),
        name="proj_in",
    )(x, g, wp, td, ti, tmi, bfp)


def _rope_table(pos, head, rot, active):
    half = rot // 2
    inv = jnp.float32(ROPE_THETA) ** (-jnp.arange(half, dtype=jnp.float32) / half)
    ang = pos.astype(jnp.float32)[:, None] * inv[None, :]
    cos, sin = jnp.cos(ang), jnp.sin(ang)
    lane = np.arange(LANES)
    jj = lane % head
    idx = jj % half
    on = lane < active
    c = jnp.where((jj < rot) & on, cos[:, idx], 1.0)
    sa = jnp.where((jj < half) & on, -sin[:, idx], 0.0)
    sb = jnp.where((jj >= half) & (jj < rot) & on, sin[:, idx], 0.0)
    return jnp.concatenate([c, sa, sb], axis=1).astype(F32)


def _rope_t(z, tab, half, period):
    f = z.shape[0]
    rep = f // period

    def tall(a):
        return a if rep == 1 else jnp.concatenate([a] * rep, axis=0)

    c = tall(tab[0:period])
    sa = tall(tab[period:2 * period])
    sb = tall(tab[2 * period:3 * period])
    up = jnp.concatenate([z[half:], z[:half]], axis=0)
    dn = jnp.concatenate([z[f - half:], z[:f - half]], axis=0)
    return z * c + up * sa + dn * sb


def _log_sigmoid(a):
    return jnp.minimum(a, 0.0) - jnp.log(1.0 + jnp.exp(-jnp.abs(a)))


_QW = (0, 512, 768, 1024, 1280, 1408)
_TW = (0, 512, 1024, 1280, 1536, 1664)


def _proj_t_kernel(x_ref, g_ref, wq_ref, wt_ref, td_ref, ti_ref, tdt_ref, tit_ref, bfc_ref,
                   qa16, qb16, qi16, u_ref, miscr_ref,
                   kat, vat, kbt, vbt, misct, ka16t, va16t, kb16t, vb16t, kir16t):
    h = _rms(x_ref[...], g_ref[...]).astype(BF16)

    def mq(i):
        return _dot(h, wq_ref[:, _QW[i]:_QW[i + 1]])

    def mt(i):
        return _dot_nt(wt_ref[_TW[i]:_TW[i + 1], :], h)

    scale = HEAD_DIM ** -0.5
    qa16[...] = (mq(0) * scale).astype(BF16)
    qb16[...] = (_rope(mq(1), td_ref[...], 8) * scale).astype(BF16)
    qi16[...] = _rope(mq(2), ti_ref[...], 4).astype(BF16)
    u_ref[...] = mq(3)
    miscr_ref[...] = mq(4)
    z = mt(0)
    kat[0] = z
    ka16t[0, 0] = z.astype(BF16)
    z = mt(1)
    vat[0] = z
    va16t[0, 0] = z.astype(BF16)
    z = _rope_t(mt(2), tdt_ref[...], 8, HEAD_DIM)
    kbt[0] = z
    kb16t[0, 0] = z.astype(BF16)
    z = mt(3)
    vbt[0] = z
    vb16t[0, 0] = z.astype(BF16)
    z = mt(4)
    ki = _rope_t(z[0:IDX_DIM], tit_ref[...], 4, IDX_DIM)
    row = lax.broadcasted_iota(I32, z.shape, 0)
    rest = jnp.where((row >= _MISC_LF) & (row < _MISC_WI), _log_sigmoid(z + bfc_ref[...]), z)
    misct[0] = jnp.concatenate([ki, rest[IDX_DIM:]], axis=0)
    kir16t[0, 0] = jnp.concatenate([ki.astype(BF16)] * IDX_HEADS, axis=0)


def _proj_t(x, g, wq, wt, td, ti, tdt, tit, bfc, b, t, tm):
    n, d = x.shape
    nt = t // tm
    row = lambda bi, ti_: (bi * nt + ti_, 0)
    const = lambda bi, ti_: (0, 0)
    tab = lambda bi, ti_: (ti_, 0)
    tabt = lambda bi, ti_: (0, ti_)
    fm = lambda bi, ti_: (bi, 0, ti_)
    ch = lambda bi, ti_: (bi, ti_, 0, 0)
    row_out = [(512, BF16), (256, BF16), (256, BF16), (256, F32), (128, F32)]
    fm_out = [512, 512, 256, 256, 128]
    ch_out = [512, 512, 256, 256, 256]
    out_shape = ([jax.ShapeDtypeStruct((n, w), dt) for w, dt in row_out]
                 + [jax.ShapeDtypeStruct((b, w, t), F32) for w in fm_out]
                 + [jax.ShapeDtypeStruct((b, nt, w, tm), BF16) for w in ch_out])
    out_specs = ([pl.BlockSpec((tm, w), row) for w, _ in row_out]
                 + [pl.BlockSpec((1, w, tm), fm) for w in fm_out]
                 + [pl.BlockSpec((1, 1, w, tm), ch) for w in ch_out])
    return pl.pallas_call(
        _proj_t_kernel,
        grid=(b, nt),
        in_specs=[pl.BlockSpec((tm, d), row), pl.BlockSpec((1, d), const),
                  pl.BlockSpec(wq.shape, const), pl.BlockSpec(wt.shape, const),
                  pl.BlockSpec((tm, 3 * LANES), tab), pl.BlockSpec((tm, 3 * LANES), tab),
                  pl.BlockSpec((3 * HEAD_DIM, tm), tabt), pl.BlockSpec((3 * IDX_DIM, tm), tabt),
                  pl.BlockSpec((LANES, 1), const)],
        out_specs=out_specs,
        out_shape=out_shape,
        compiler_params=_cparams(("arbitrary", "arbitrary")),
        name="proj_in_prompt",
    )(x, g, wq, wt, td, ti, tdt, tit, bfc)


def _rope_table_t(pos, head, rot):
    half = rot // 2
    inv = jnp.float32(ROPE_THETA) ** (-jnp.arange(half, dtype=jnp.float32) / half)
    ang = pos.astype(jnp.float32)[:, None] * inv[None, :]
    cos, sin = jnp.cos(ang).T, jnp.sin(ang).T
    jj = np.arange(head)
    idx = jj % half
    c = jnp.where((jj < rot)[:, None], cos[idx], 1.0)
    sa = jnp.where((jj < half)[:, None], -sin[idx], 0.0)
    sb = jnp.where(((jj >= half) & (jj < rot))[:, None], sin[idx], 0.0)
    return jnp.concatenate([c, sa, sb], axis=0).astype(F32)


def _nmm_t_kernel(x_ref, g_ref, wt_ref, o_ref, o16_ref):
    z = _dot_nt(wt_ref[...], _rms(x_ref[...], g_ref[...]).astype(BF16))
    o_ref[0] = z
    o16_ref[0] = z.astype(BF16)


def _norm_matmul_t(x, g, wt16, nblk, tm):
    d = x.shape[1]
    m = wt16.shape[0]
    return pl.pallas_call(
        _nmm_t_kernel,
        grid=(nblk,),
        in_specs=[pl.BlockSpec((tm, d), lambda i: (i, 0)), pl.BlockSpec((1, d), lambda i: (0, 0)),
                  pl.BlockSpec((m, d), lambda i: (0, 0))],
        out_specs=[pl.BlockSpec((1, m, tm), lambda i: (i, 0, 0)), pl.BlockSpec((1, m, tm), lambda i: (i, 0, 0))],
        out_shape=[jax.ShapeDtypeStruct((nblk, m, tm), F32), jax.ShapeDtypeStruct((nblk, m, tm), BF16)],
        compiler_params=_cparams(("arbitrary",)),
        name="norm_matmul_t",
    )(x, g, wt16)


def _fox_kernel(q_ref, kt_ref, vt_ref, cft_ref, o_ref, *, ch):
    hp = pl.program_id(1)
    qt = pl.program_id(2)
    q = q_ref[...]
    qms = [q * _head_mask((1, LANES), i, HEAD_DIM, BF16) for i in range(2)]
    causal = (lax.broadcasted_iota(I32, (ch, ch), 1) <= lax.broadcasted_iota(I32, (ch, ch), 0))

    def step(kt, carry, masked):
        kk = kt_ref[0, kt]
        vv = vt_ref[0, kt]
        out = []
        for i in range(2):
            m, l, acc = carry[3 * i:3 * i + 3]
            ck = cft_ref[0, pl.ds(2 * hp + i, 1), pl.ds(kt, 1), :].reshape(1, ch)
            s = _dot(qms[i], kk) - ck
            if masked:
                s = jnp.where(causal, s, NEG)
            m_new = jnp.maximum(m, jnp.max(s, axis=1, keepdims=True))
            alpha = jnp.exp(m - m_new)
            p = jnp.exp(s - m_new)
            l = alpha * l + jnp.sum(p, axis=1, keepdims=True)
            acc = alpha * acc + _dot_nt(p.astype(BF16), vv)
            out += [m_new, l, acc]
        return tuple(out)

    init = (jnp.full((ch, 1), NEG, F32), jnp.zeros((ch, 1), F32), jnp.zeros((ch, LANES), F32)) * 2
    carry = lax.fori_loop(0, qt, functools.partial(step, masked=False), init)
    carry = step(qt, carry, True)
    lane = lax.broadcasted_iota(I32, (ch, LANES), 1)
    o_ref[...] = jnp.where(lane < HEAD_DIM, carry[2] / carry[1], carry[5] / carry[4]).astype(BF16)


def _fox_prompt(qa16, ka16t, va16t, cft4, b, t, ch):
    nq = t // ch
    n = b * t
    kv = pl.BlockSpec((1, nq, LANES, ch), lambda bi, hp, qi: (bi, 0, hp, 0))
    return pl.pallas_call(
        functools.partial(_fox_kernel, ch=ch),
        grid=(b, FOX_HEADS // 2, nq),
        in_specs=[pl.BlockSpec((ch, LANES), lambda bi, hp, qi: (bi * nq + qi, hp)), kv, kv,
                  pl.BlockSpec((1, FOX_HEADS, nq, ch), lambda bi, hp, qi: (bi, 0, 0, 0))],
        out_specs=pl.BlockSpec((ch, LANES), lambda bi, hp, qi: (bi * nq + qi, hp)),
        out_shape=jax.ShapeDtypeStruct((n, FOX_HEADS * HEAD_DIM), BF16),
        compiler_params=_cparams(("arbitrary", "arbitrary", "arbitrary")),
        name="fox_prompt",
    )(qa16, ka16t, va16t, cft4)


def _sort_key(score, valid):
    bits = lax.bitcast_convert_type(score, I32)
    key = bits ^ (lax.shift_right_arithmetic(bits, 31) & 0x7FFFFFFF)
    key = jnp.where(score == 0.0, 0, key)
    return jnp.where(valid, key, INT_MIN)


def _select_threshold(keys_ref, nch, rows, width, ksel):
    nsl = width // LANES
    kf = float(ksel)

    def count_ge(cand):
        def body(c, acc):
            m = jnp.where(keys_ref[c] >= cand, 1.0, 0.0)
            for s in range(nsl):
                acc = acc + m[:, s * LANES:(s + 1) * LANES]
            return acc

        acc = lax.fori_loop(0, nch, body, jnp.zeros((rows, LANES), F32))
        return jnp.sum(acc, axis=1, keepdims=True)

    tau = jnp.where(count_ge(jnp.zeros((rows, 1), I32)) >= kf, 0, INT_MIN).astype(I32)

    def bit_body(bi, tau):
        cand = tau | lax.shift_left(jnp.int32(1), 30 - bi)
        return jnp.where(count_ge(cand) >= kf, cand, tau)

    tau = lax.fori_loop(0, 31, bit_body, tau)
    tie = (tau > INT_MIN) & (count_ge(tau) > kf)

    @pl.when(jnp.max(jnp.where(tie, 1.0, 0.0)) > 0.0)
    def _():
        need = kf - count_ge(tau + 1)
        r_ = lax.broadcasted_iota(I32, (LANES, LANES), 0)
        c_ = lax.broadcasted_iota(I32, (LANES, LANES), 1)
        tri = jnp.where(r_ < c_, 1.0, 0.0).astype(BF16)

        def body(c, carry):
            for s in range(nsl):
                kk = keys_ref[c, :, s * LANES:(s + 1) * LANES]
                eq = kk == tau
                eqf = jnp.where(eq, 1.0, 0.0)
                pre = _dot(eqf.astype(BF16), tri) + carry
                drop = eq & (pre >= need) & tie
                keys_ref[c, :, s * LANES:(s + 1) * LANES] = jnp.where(drop, INT_MIN, kk)
                carry = carry + jnp.sum(eqf, axis=1, keepdims=True)
            return carry

        lax.fori_loop(0, nch, body, jnp.zeros((rows, 1), F32))

    return jnp.maximum(tau, INT_MIN + 1)


def _dsa_kernel(qi_ref, misc_ref, kir_ref, q_ref, kt_ref, vt_ref, o_ref, keys_ref, *, tq, w, n_sel):
    qt = pl.program_id(1)
    nch = ((qt + 1) * tq + w - 1) // w
    qi = qi_ref[...]
    misc = misc_ref[...]
    rowpos = qt * tq + lax.broadcasted_iota(I32, (tq, w), 0)
    coli = lax.broadcasted_iota(I32, (tq, w), 1)
    qms = [qi * _head_mask((1, IDX_HEADS * IDX_DIM), j, IDX_DIM, BF16) for j in range(IDX_HEADS)]
    wcols = [misc[:, _MISC_WI + j:_MISC_WI + j + 1] for j in range(IDX_HEADS)]

    def score_chunk(c, carry):
        kk = kir_ref[0, c]
        sc = jnp.zeros((tq, w), F32)
        for j in range(IDX_HEADS):
            sc = sc + wcols[j] * jnp.maximum(_dot(qms[j], kk), 0.0)
        keys_ref[c] = _sort_key(sc, c * w + coli <= rowpos)
        return carry

    lax.fori_loop(0, nch, score_chunk, 0)
    tau = _select_threshold(keys_ref, nch, tq, w, n_sel)

    q = q_ref[...]
    nl = DSA_HEADS * HEAD_DIM
    hms = [_head_mask((1, nl), h, HEAD_DIM, F32) for h in range(DSA_HEADS)]
    qhs = [q * hm.astype(BF16) for hm in hms]

    def step(c, carry):
        kk = kt_ref[0, c]
        vv = vt_ref[0, c]
        sel = keys_ref[c] >= tau
        out = []
        for h in range(DSA_HEADS):
            m, l, acc = carry[3 * h:3 * h + 3]
            s = jnp.where(sel, _dot(qhs[h], kk), NEG)
            m_new = jnp.maximum(m, jnp.max(s, axis=1, keepdims=True))
            alpha = jnp.exp(m - m_new)
            p = jnp.exp(s - m_new)
            l = alpha * l + jnp.sum(p, axis=1, keepdims=True)
            acc = alpha * acc + _dot_nt(p.astype(BF16), vv)
            out += [m_new, l, acc]
        return tuple(out)

    init = (jnp.full((tq, 1), NEG, F32), jnp.zeros((tq, 1), F32), jnp.zeros((tq, nl), F32)) * DSA_HEADS
    carry = lax.fori_loop(0, nch, step, init)
    out = jnp.zeros((tq, nl), F32)
    for h in range(DSA_HEADS):
        out = out + (carry[3 * h + 2] / carry[3 * h + 1]) * hms[h]
    o_ref[...] = out.astype(BF16)


def _dsa_prompt(qi16, miscr, kir16t, qb16, kb16t, vb16t, b, t, tq, w):
    nq = t // tq
    n = b * t
    nl = DSA_HEADS * HEAD_DIM
    n_sel = min(TOPK_MAX, t // 4)
    qrow = lambda bi, qi: (bi * nq + qi, 0)
    kv = pl.BlockSpec((1, t // w, nl, w), lambda bi, qi: (bi, 0, 0, 0))
    return pl.pallas_call(
        functools.partial(_dsa_kernel, tq=tq, w=w, n_sel=n_sel),
        grid=(b, nq),
        in_specs=[pl.BlockSpec((tq, nl), qrow), pl.BlockSpec((tq, LANES), qrow), kv,
                  pl.BlockSpec((tq, nl), qrow), kv, kv],
        out_specs=pl.BlockSpec((tq, nl), qrow),
        out_shape=jax.ShapeDtypeStruct((n, nl), BF16),
        scratch_shapes=[pltpu.VMEM((t // w, tq, w), I32)],
        compiler_params=_cparams(("arbitrary", "arbitrary")),
        name="dsa_prompt",
    )(qi16, miscr, kir16t, qb16, kb16t, vb16t)


def _pool_mix(win_sums, x, cnts, wblk, scale):
    lane = lax.broadcasted_iota(I32, x.shape, 1)
    mean = jnp.zeros_like(x)
    for g, wdw in enumerate(POOL_WINDOWS):
        mg = win_sums[wdw] / cnts[g]
        mean = jnp.where((lane >= g * POOL_CH) & (lane < (g + 1) * POOL_CH), mg, mean)
    return (_dot((mean - x).astype(BF16), wblk) * scale).astype(BF16)


def _pool_kernel(u_ref, wblk_ref, scale_ref, o_ref, ext_ref, *, tt):
    t = pl.program_id(1)
    hist = POOL_WINDOWS[-1]

    @pl.when(t == 0)
    def _():
        ext_ref[0:hist, :] = jnp.zeros((hist, ext_ref.shape[1]), F32)

    x = u_ref[...]
    ext_ref[hist:hist + tt, :] = x
    acc = x
    sums = {}
    for s in range(1, hist):
        acc = acc + ext_ref[hist - s:hist - s + tt, :]
        if s + 1 in POOL_WINDOWS:
            sums[s + 1] = acc
    pos = t * tt + lax.broadcasted_iota(I32, (tt, 1), 0)
    cnts = [jnp.minimum(pos + 1, wdw).astype(F32) for wdw in POOL_WINDOWS]
    o_ref[...] = _pool_mix(sums, x, cnts, wblk_ref[...], scale_ref[...])
    ext_ref[0:hist, :] = ext_ref[tt:tt + hist, :]


def _pool_prompt(u, wblk16, scale, b, t, tt):
    n, c = u.shape
    nt = t // tt
    return pl.pallas_call(
        functools.partial(_pool_kernel, tt=tt),
        grid=(b, nt),
        in_specs=[pl.BlockSpec((tt, c), lambda bi, ti: (bi * nt + ti, 0)),
                  pl.BlockSpec((c, c), lambda bi, ti: (0, 0)),
                  pl.BlockSpec((1, c), lambda bi, ti: (0, 0))],
        out_specs=pl.BlockSpec((tt, c), lambda bi, ti: (bi * nt + ti, 0)),
        out_shape=jax.ShapeDtypeStruct((n, c), BF16),
        scratch_shapes=[pltpu.VMEM((tt + POOL_WINDOWS[-1], c), F32)],
        compiler_params=_cparams(("arbitrary", "arbitrary")),
        name="pool_prompt",
    )(u, wblk16, scale)


def _pool_s_kernel(st_ref, u_ref, wblk_ref, scale_ref, o_ref):
    x = u_ref[...]
    acc = x
    sums = {}
    for s in range(1, POOL_WINDOWS[-1]):
        acc = acc + st_ref[POOL_BUF - s]
        if s + 1 in POOL_WINDOWS:
            sums[s + 1] = acc
    cnts = [float(wdw) for wdw in POOL_WINDOWS]
    o_ref[...] = _pool_mix(sums, x, cnts, wblk_ref[...], scale_ref[...])


def _pool_sample(state_t, u, wblk16, scale):
    n, c = u.shape
    return pl.pallas_call(
        _pool_s_kernel,
        out_shape=jax.ShapeDtypeStruct((n, c), BF16),
        name="pool_sample",
    )(state_t, u, wblk16, scale)


def _mix_out(x, oa, ob, oc, wo_ref):
    na, nb = oa.shape[1], ob.shape[1]
    return (x + _dot(oa, wo_ref[0:na, :]) + _dot(ob, wo_ref[na:na + nb, :])
            + _dot(oc, wo_ref[na + nb:, :]))


def _mixx_kernel(x_ref, oa_ref, ob_ref, oc_ref, wo_ref, gx_ref, wxq_ref, mk_ref, mv_ref, wxo_ref,
                 o_ref):
    x1 = _mix_out(x_ref[...], oa_ref[...], ob_ref[...], oc_ref[...], wo_ref)
    h = _rms(x1, gx_ref[...]).astype(BF16)
    q = (_dot(h, wxq_ref[...]) * HEAD_DIM ** -0.5).astype(BF16)
    mk = mk_ref[0]
    mv = mv_ref[0]
    nl = X_HEADS * HEAD_DIM
    o = jnp.zeros(q.shape, F32)
    for hh in range(X_HEADS):
        hm = _head_mask((1, nl), hh, HEAD_DIM, F32)
        s = _dot(q * hm.astype(BF16), mk)
        p = jnp.exp(s - jnp.max(s, axis=1, keepdims=True))
        l = jnp.sum(p, axis=1, keepdims=True)
        o = o + (_dot_nt(p.astype(BF16), mv) / l) * hm
    o_ref[...] = x1 + _dot(o.astype(BF16), wxo_ref[...])


def _mixx_prompt(x, oa, ob, oc, wo16, gx, wxq16, mk16, mv16, wxo16, t, tm):
    n, d = x.shape
    per_b = t // tm
    row = lambda i: (i, 0)
    const = lambda i: (0, 0)
    mem = lambda i: (i // per_b, 0, 0)
    return pl.pallas_call(
        _mixx_kernel,
        grid=(n // tm,),
        in_specs=[pl.BlockSpec((tm, d), row), pl.BlockSpec((tm, oa.shape[1]), row),
                  pl.BlockSpec((tm, ob.shape[1]), row), pl.BlockSpec((tm, oc.shape[1]), row),
                  pl.BlockSpec(wo16.shape, const), pl.BlockSpec((1, d), const),
                  pl.BlockSpec(wxq16.shape, const),
                  pl.BlockSpec((1,) + mk16.shape[1:], mem), pl.BlockSpec((1,) + mv16.shape[1:], mem),
                  pl.BlockSpec(wxo16.shape, const)],
        out_specs=pl.BlockSpec((tm, d), row),
        out_shape=jax.ShapeDtypeStruct((n, d), F32),
        compiler_params=_cparams(("arbitrary",)),
        name="mix_xattn_prompt",
    )(x, oa, ob, oc, wo16, gx, wxq16, mk16, mv16, wxo16)


def _rows_by_head(v, nrows, width):
    lanes = v.shape[1]
    r = lax.broadcasted_iota(I32, (nrows, lanes), 0)
    lane = lax.broadcasted_iota(I32, (nrows, lanes), 1)
    keep = (lane >= r * width) & (lane < (r + 1) * width)
    return jnp.where(keep, jnp.broadcast_to(v, (nrows, lanes)), 0.0), keep


def _mixx_s_kernel(x_ref, oa_ref, ob_ref, oc_ref, wo_ref, gx_ref, wxq_ref, mk_ref, mv_ref, wxo_ref,
                   o_ref, att_ref, *, sb):
    x1 = _mix_out(x_ref[...], oa_ref[...], ob_ref[...], oc_ref[...], wo_ref)
    h = _rms(x1, gx_ref[...]).astype(BF16)
    q = _dot(h, wxq_ref[...]) * HEAD_DIM ** -0.5
    for b in range(sb):
        qrows, keep = _rows_by_head(q[b:b + 1, :], 8, HEAD_DIM)
        s = _dot(qrows.astype(BF16), mk_ref[0, b].astype(BF16))
        p = jnp.exp(s - jnp.max(s, axis=1, keepdims=True))
        l = jnp.sum(p, axis=1, keepdims=True)
        ob_ = _dot_nt(p.astype(BF16), mv_ref[0, b].astype(BF16)) / l
        att_ref[b:b + 1, :] = jnp.sum(jnp.where(keep, ob_, 0.0), axis=0, keepdims=True)
    o_ref[...] = x1 + _dot(att_ref[...].astype(BF16), wxo_ref[...])


def _mixx_sample(x, oa, ob, oc, wo16, gx, wxq16, memk, memv, layer, wxo16, sb):
    n, d = x.shape
    nl = X_HEADS * HEAD_DIM
    row = lambda i: (i, 0)
    const = lambda i: (0, 0)
    mem = lambda i: (layer, i, 0, 0)
    mblk = (1, sb) + memk.shape[2:]
    return pl.pallas_call(
        functools.partial(_mixx_s_kernel, sb=sb),
        grid=(n // sb,),
        in_specs=[pl.BlockSpec((sb, d), row), pl.BlockSpec((sb, oa.shape[1]), row),
                  pl.BlockSpec((sb, ob.shape[1]), row), pl.BlockSpec((sb, oc.shape[1]), row),
                  pl.BlockSpec(wo16.shape, const), pl.BlockSpec((1, d), const),
                  pl.BlockSpec(wxq16.shape, const),
                  pl.BlockSpec(mblk, mem), pl.BlockSpec(mblk, mem),
                  pl.BlockSpec(wxo16.shape, const)],
        out_specs=pl.BlockSpec((sb, d), row),
        out_shape=jax.ShapeDtypeStruct((n, d), F32),
        scratch_shapes=[pltpu.VMEM((sb, nl), F32)],
        compiler_params=_cparams(("arbitrary",)),
        name="mix_xattn_sample",
    )(x, oa, ob, oc, wo16, gx, wxq16, memk, memv, wxo16)


def _ffn_kernel(eid_ref, x_ref, g_ref, wg_ref, wu_ref, wd_ref, o_ref, h_ref, acc_ref, *, dense):
    j = pl.program_id(1)

    @pl.when(j == 0)
    def _():
        if dense:
            h_ref[...] = _rms(x_ref[...], g_ref[...]).astype(BF16)
        else:
            h_ref[...] = x_ref[...]
        acc_ref[...] = jnp.zeros(acc_ref.shape, F32)

    h = h_ref[...]
    a = _dot(h, wg_ref[0])
    b = _dot(h, wu_ref[0])
    t = (a * (1.0 / (1.0 + jnp.exp(-a)))) * b
    acc_ref[...] += _dot(t.astype(BF16), wd_ref[0])

    @pl.when(j == pl.num_programs(1) - 1)
    def _():
        if dense:
            o_ref[...] = x_ref[...] + acc_ref[...]
        else:
            o_ref[...] = acc_ref[...]


def _ffn(eid, x, g, wg16, wu16, wd16, bm, tf, dense):
    n, d = x.shape
    dff = wg16.shape[2]
    grid = (n // bm, dff // tf)
    gs = pltpu.PrefetchScalarGridSpec(
        num_scalar_prefetch=1,
        grid=grid,
        in_specs=[pl.BlockSpec((bm, d), lambda i, j, e: (i, 0)),
                  pl.BlockSpec((1, d), lambda i, j, e: (0, 0)),
                  pl.BlockSpec((1, d, tf), lambda i, j, e: (e[i], 0, j)),
                  pl.BlockSpec((1, d, tf), lambda i, j, e: (e[i], 0, j)),
                  pl.BlockSpec((1, tf, d), lambda i, j, e: (e[i], j, 0))],
        out_specs=pl.BlockSpec((bm, d), lambda i, j, e: (i, 0)),
        scratch_shapes=[pltpu.VMEM((bm, d), BF16), pltpu.VMEM((bm, d), F32)],
    )
    return pl.pallas_call(
        functools.partial(_ffn_kernel, dense=dense),
        grid_spec=gs,
        out_shape=jax.ShapeDtypeStruct((n, d), F32),
        compiler_params=_cparams(("arbitrary", "arbitrary")),
        name="ffn_dense" if dense else "ffn_experts",
    )(eid, x, g, wg16, wu16, wd16)


def _router_kernel(x_ref, g_ref, wr_ref, h_ref, info_ref):
    h = _rms(x_ref[...], g_ref[...]).astype(BF16)
    h_ref[...] = h
    logits = _dot(h, wr_ref[...])
    lane = lax.broadcasted_iota(I32, logits.shape, 1)
    logits = jnp.where(lane < N_EXPERTS, logits, NEG)
    m1 = jnp.max(logits, axis=1, keepdims=True)
    i1 = jnp.min(jnp.where(logits == m1, lane, LANES), axis=1, keepdims=True)
    rest = jnp.where(lane == i1, NEG, logits)
    m2 = jnp.max(rest, axis=1, keepdims=True)
    i2 = jnp.min(jnp.where(rest == m2, lane, LANES), axis=1, keepdims=True)
    e = jnp.exp(m2 - m1)
    g1 = 1.0 / (1.0 + e)
    g2 = e / (1.0 + e)
    info = jnp.where(lane == 0, i1.astype(F32), jnp.where(lane == 1, i2.astype(F32),
                     jnp.where(lane == 2, g1, jnp.where(lane == 3, g2, 0.0))))
    info_ref[...] = info


def _router(x, g, wr16, tm):
    n, d = x.shape
    return pl.pallas_call(
        _router_kernel,
        grid=(n // tm,),
        in_specs=[pl.BlockSpec((tm, d), lambda i: (i, 0)), pl.BlockSpec((1, d), lambda i: (0, 0)),
                  pl.BlockSpec((d, LANES), lambda i: (0, 0))],
        out_specs=[pl.BlockSpec((tm, d), lambda i: (i, 0)), pl.BlockSpec((tm, LANES), lambda i: (i, 0))],
        out_shape=[jax.ShapeDtypeStruct((n, d), BF16), jax.ShapeDtypeStruct((n, LANES), F32)],
        compiler_params=_cparams(("arbitrary",)),
        name="moe_router",
    )(x, g, wr16)


def _moe(x, g, wr16, wg16, wu16, wd16, tm, bm, tf):
    n, d = x.shape
    h16, info = _router(x, g, wr16, tm)
    top_e = info[:, 0:2].astype(I32)
    gates = info[:, 2:4]
    n_as = 2 * n
    flat_e = top_e.reshape(n_as)
    order = jnp.argsort(flat_e)
    se = flat_e[order]
    stok = (order // 2).astype(I32)
    counts = jnp.bincount(flat_e, length=N_EXPERTS)
    padded = (counts + bm - 1) // bm * bm
    pad_end = jnp.cumsum(padded)
    pad_start = pad_end - padded
    grp_start = jnp.cumsum(counts) - counts
    dest_sorted = (pad_start[se] + jnp.arange(n_as, dtype=I32) - grp_start[se]).astype(I32)
    nblk = -(-n_as // bm) + N_EXPERTS
    row_tok = jnp.zeros((nblk * bm,), I32).at[dest_sorted].set(stok)
    blk_e = jnp.minimum(jnp.searchsorted(pad_end, jnp.arange(nblk) * bm, side='right'),
                        N_EXPERTS - 1).astype(I32)
    xs = h16[row_tok]
    ys = _ffn(blk_e, xs, g, wg16, wu16, wd16, bm, tf, dense=False)
    dest = jnp.zeros((n_as,), I32).at[order].set(dest_sorted)
    contrib = ys[dest].reshape(n, 2, d) * gates[:, :, None]
    return x + contrib[:, 0] + contrib[:, 1]


def _rmsout_kernel(x_ref, g_ref, o_ref):
    o_ref[...] = _rms(x_ref[...], g_ref[...])


def _rms_out(x, g, tm):
    n, d = x.shape
    return pl.pallas_call(
        _rmsout_kernel,
        grid=(n // tm,),
        in_specs=[pl.BlockSpec((tm, d), lambda i: (i, 0)), pl.BlockSpec((1, d), lambda i: (0, 0))],
        out_specs=pl.BlockSpec((tm, d), lambda i: (i, 0)),
        out_shape=jax.ShapeDtypeStruct((n, d), F32),
        compiler_params=_cparams(("arbitrary",)),
        name="final_norm",
    )(x, g)


def _dsa_score_s_kernel(pt_ref, qi_ref, wi_ref, kn_ref, *rest, n_pages, page):
    pages = rest[:n_pages]
    o_ref = rest[n_pages]
    qi = qi_ref[0]
    wi = wi_ref[0]
    slabs = []
    for p in range(n_pages):
        r = _dot(qi, pages[p][0, 0].astype(BF16))
        slabs.append(jnp.sum(wi * jnp.maximum(r, 0.0), axis=0, keepdims=True))
    kn = kn_ref[0].astype(BF16).astype(F32)
    rn = jnp.sum(qi.astype(F32) * kn, axis=1, keepdims=True)
    sn = jnp.sum(wi * jnp.maximum(rn, 0.0), axis=0, keepdims=True)
    lane = lax.broadcasted_iota(I32, (1, page), 1)
    slabs.append(jnp.where(lane == 0, sn, NEG))
    o_ref[0] = jnp.concatenate(slabs, axis=1)


def _dsa_score_sample(page_table, qi3, wi3, kinew, cache_kidx, layer):
    nb, n_pages = page_table.shape
    page = cache_kidx.shape[3]
    width = (n_pages + 1) * page
    pspecs = [pl.BlockSpec((1, 1, IDX_DIM, page), functools.partial(
        lambda b, pt, p: (layer, pt[b, p], 0, 0), p=p)) for p in range(n_pages)]
    gs = pltpu.PrefetchScalarGridSpec(
        num_scalar_prefetch=1,
        grid=(nb,),
        in_specs=[pl.BlockSpec((1, IDX_HEADS, IDX_DIM), lambda b, pt: (b, 0, 0)),
                  pl.BlockSpec((1, IDX_HEADS, 1), lambda b, pt: (b, 0, 0)),
                  pl.BlockSpec((1, 1, IDX_DIM), lambda b, pt: (b, 0, 0))] + pspecs,
        out_specs=pl.BlockSpec((1, 1, width), lambda b, pt: (b, 0, 0)),
    )
    return pl.pallas_call(
        functools.partial(_dsa_score_s_kernel, n_pages=n_pages, page=page),
        grid_spec=gs,
        out_shape=jax.ShapeDtypeStruct((nb, 1, width), F32),
        compiler_params=_cparams(("arbitrary",)),
        name="dsa_score_sample",
    )(page_table, qi3, wi3, kinew, *([cache_kidx] * n_pages))


def _dsa_select_s_kernel(s_ref, o_ref, keys_ref, *, n_valid, n_sel):
    nch, rows, width = keys_ref.shape
    coli = lax.broadcasted_iota(I32, (rows, width), 1)
    for c in range(nch):
        sc = s_ref[:, c * width:(c + 1) * width]
        keys_ref[c] = _sort_key(sc, c * width + coli < n_valid)
    tau = _select_threshold(keys_ref, nch, rows, width, n_sel)
    for c in range(nch):
        o_ref[:, c * width:(c + 1) * width] = jnp.where(keys_ref[c] >= tau, 0.0, NEG)


def _dsa_select_sample(scores, n_valid, n_sel):
    rows, width = scores.shape
    return pl.pallas_call(
        functools.partial(_dsa_select_s_kernel, n_valid=n_valid, n_sel=n_sel),
        out_shape=jax.ShapeDtypeStruct((rows, width), F32),
        scratch_shapes=[pltpu.VMEM((width // LANES, rows, LANES), I32)],
        name="dsa_select_sample",
    )(scores)


def _decode_attend(q, knew, vnew, k_pages, v_pages, bias_fn, bias_new, nrows):
    page = k_pages[0].shape[3]
    qrows, keep = _rows_by_head(q.astype(F32), nrows, HEAD_DIM)
    q16 = qrows.astype(BF16)
    s = jnp.concatenate(
        [_dot(q16, kp[0, 0].astype(BF16)) + bias_fn(p) for p, kp in enumerate(k_pages)], axis=1)
    kn = knew.astype(BF16).astype(F32)
    sn = jnp.sum(qrows * kn, axis=1, keepdims=True) + bias_new
    m = jnp.maximum(jnp.max(s, axis=1, keepdims=True), sn)
    pr = jnp.exp(s - m)
    pn = jnp.exp(sn - m)
    l = jnp.sum(pr, axis=1, keepdims=True) + pn
    acc = pn * vnew.astype(BF16).astype(F32)
    for p, vp in enumerate(v_pages):
        acc = acc + _dot_nt(pr[:, p * page:(p + 1) * page].astype(BF16), vp[0, 0].astype(BF16))
    return jnp.sum(jnp.where(keep, acc / l, 0.0), axis=0, keepdims=True)


def _decode_kernel(pt_ref, qa_ref, qb_ref, kan_ref, van_ref, kbn_ref, vbn_ref, fb_ref, dm_ref, *rest,
                   n_pages, page):
    fk = rest[0:n_pages]
    fv = rest[n_pages:2 * n_pages]
    dk = rest[2 * n_pages:3 * n_pages]
    dv = rest[3 * n_pages:4 * n_pages]
    oa_ref, ob_ref = rest[4 * n_pages:]
    oa = _decode_attend(qa_ref[0], kan_ref[0], van_ref[0], fk, fv,
                        lambda p: fb_ref[0, p], 0.0, FOX_HEADS)
    oa_ref[0] = oa.astype(BF16)
    ob = _decode_attend(qb_ref[0], kbn_ref[0], vbn_ref[0], dk, dv,
                        lambda p: dm_ref[0, :, p * page:(p + 1) * page],
                        dm_ref[0, :, n_pages * page:n_pages * page + 1], 8)
    ob_ref[0] = ob.astype(BF16)


def _decode_attention(page_table, qa3, qb3, kan, van, kbn, vbn, fbias, dmask,
                      cache_fk, cache_fv, cache_dk, cache_dv, layer):
    nb, n_pages = page_table.shape
    page = cache_fk.shape[3]
    wa = FOX_HEADS * HEAD_DIM
    wb = DSA_HEADS * HEAD_DIM

    def pspecs(width):
        return [pl.BlockSpec((1, 1, width, page), functools.partial(
            lambda b, pt, p: (layer, pt[b, p], 0, 0), p=p)) for p in range(n_pages)]

    vec = lambda w: pl.BlockSpec((1, 1, w), lambda b, pt: (b, 0, 0))
    gs = pltpu.PrefetchScalarGridSpec(
        num_scalar_prefetch=1,
        grid=(nb,),
        in_specs=[vec(wa), vec(wb), vec(wa), vec(wa), vec(wb), vec(wb),
                  pl.BlockSpec((1, n_pages, FOX_HEADS, page), lambda b, pt: (b, 0, 0, 0)),
                  vec(dmask.shape[2])] + pspecs(wa) + pspecs(wa) + pspecs(wb) + pspecs(wb),
        out_specs=[vec(wa), vec(wb)],
    )
    return pl.pallas_call(
        functools.partial(_decode_kernel, n_pages=n_pages, page=page),
        grid_spec=gs,
        out_shape=[jax.ShapeDtypeStruct((nb, 1, wa), BF16), jax.ShapeDtypeStruct((nb, 1, wb), BF16)],
        compiler_params=_cparams(("arbitrary",)),
        name="decode_attention",
    )(page_table, qa3, qb3, kan, van, kbn, vbn, fbias, dmask,
      *([cache_fk] * n_pages + [cache_fv] * n_pages + [cache_dk] * n_pages + [cache_dv] * n_pages))


def _prep_w_in(w_in):
    fw, dw = FOX_HEADS * HEAD_DIM, DSA_HEADS * HEAD_DIM
    o_fa = 3 * fw
    o_qb = o_fa + FOX_HEADS
    o_qi = o_qb + 3 * dw
    o_ki = o_qi + IDX_HEADS * IDX_DIM
    o_wi = o_ki + IDX_DIM
    o_u = o_wi + IDX_HEADS
    d = w_in.shape[0]
    misc = jnp.concatenate([w_in[:, o_ki:o_wi], w_in[:, o_fa:o_qb], w_in[:, o_wi:o_u],
                            jnp.zeros((d, LANES - IDX_DIM - FOX_HEADS - IDX_HEADS), w_in.dtype)], axis=1)
    return jnp.concatenate([w_in[:, 0:o_fa], w_in[:, o_qb:o_qi], w_in[:, o_qi:o_ki], misc,
                            w_in[:, o_u:]], axis=1).astype(BF16)


def _prep_w_in_t(w_in):
    fw, dw = FOX_HEADS * HEAD_DIM, DSA_HEADS * HEAD_DIM
    o_fa = 3 * fw
    o_qb = o_fa + FOX_HEADS
    o_kb = o_qb + dw
    o_qi = o_qb + 3 * dw
    o_ki = o_qi + IDX_HEADS * IDX_DIM
    o_wi = o_ki + IDX_DIM
    o_u = o_wi + IDX_HEADS
    d = w_in.shape[0]
    misc = jnp.concatenate([w_in[:, o_ki:o_wi], w_in[:, o_fa:o_qb], w_in[:, o_wi:o_u],
                            jnp.zeros((d, LANES - IDX_DIM - FOX_HEADS - IDX_HEADS), w_in.dtype)], axis=1)
    wq = jnp.concatenate([w_in[:, 0:fw], w_in[:, o_qb:o_kb], w_in[:, o_qi:o_ki], w_in[:, o_u:], misc],
                         axis=1).astype(BF16)
    wt = jnp.concatenate([w_in[:, fw:o_fa], w_in[:, o_kb:o_qi], misc], axis=1).T.astype(BF16)
    return wq, wt


def _block_diag(w_pool):
    g, c, _ = w_pool.shape
    out = jnp.zeros((g * c, g * c), w_pool.dtype)
    for i in range(g):
        out = out.at[i * c:(i + 1) * c, i * c:(i + 1) * c].set(w_pool[i])
    return out.astype(BF16)


def kernel(x_prompt, x_sample, cache_fox_k, cache_fox_v, cache_fox_logf, cache_dsa_k, cache_dsa_v,
           cache_dsa_kidx, state_pool, cache_mem_k, cache_mem_v, page_table, mem_prompt,
           norm_mix, w_in, b_forget, w_pool, pool_scale, w_out, norm_x, norm_mem, w_xq, w_xkv, w_xo,
           norm_ffn, ffn_w_gate, ffn_w_up, ffn_w_down, moe_router, moe_w_gate, moe_w_up, moe_w_down,
           norm_final):
    b, t, d = x_prompt.shape
    nb = x_sample.shape[0]
    depth = w_in.shape[0]
    n_pool, page = cache_fox_k.shape[1], cache_fox_k.shape[2]
    n_pages = page_table.shape[1]
    n_past = n_pages * page
    n_mem = mem_prompt.shape[1]
    fw, dw = FOX_HEADS * HEAD_DIM, DSA_HEADS * HEAD_DIM
    xw = X_HEADS * HEAD_DIM
    n = b * t

    ch, tq_dsa, tt_pool, tm_mix = 512, 256, 512, 512
    tm_ffn, tf_dense, tm_moe, bm_moe, tf_moe = 1024, 256, 512, 512, 512
    nt = t // ch

    pos_p = jnp.arange(t, dtype=I32)
    pos_s = jnp.full((nb,), n_past, I32)
    tab_p = (_rope_table(pos_p, HEAD_DIM, HEAD_DIM // 4, LANES), _rope_table(pos_p, IDX_DIM, IDX_DIM // 4, LANES),
             _rope_table_t(pos_p, HEAD_DIM, HEAD_DIM // 4), _rope_table_t(pos_p, IDX_DIM, IDX_DIM // 4))
    tab_s = (_rope_table(pos_s, HEAD_DIM, HEAD_DIM // 4, LANES), _rope_table(pos_s, IDX_DIM, IDX_DIM // 4, LANES),
             _rope_table(pos_s, IDX_DIM, IDX_DIM // 4, IDX_DIM))

    cfk = jnp.transpose(cache_fox_k, (0, 1, 3, 4, 2)).reshape(depth, n_pool, fw, page)
    cfv = jnp.transpose(cache_fox_v, (0, 1, 3, 4, 2)).reshape(depth, n_pool, fw, page)
    cdk = jnp.transpose(cache_dsa_k, (0, 1, 3, 4, 2)).reshape(depth, n_pool, dw, page)
    cdv = jnp.transpose(cache_dsa_v, (0, 1, 3, 4, 2)).reshape(depth, n_pool, dw, page)
    ckidx = jnp.transpose(cache_dsa_kidx, (0, 1, 3, 2))
    clogf = jnp.transpose(cache_fox_logf, (0, 1, 3, 2))
    cmk = jnp.transpose(cache_mem_k, (0, 1, 3, 4, 2)).reshape(depth, nb, xw, n_mem)
    cmv = jnp.transpose(cache_mem_v, (0, 1, 3, 4, 2)).reshape(depth, nb, xw, n_mem)
    pool_t = jnp.transpose(state_pool, (0, 2, 1, 3))
    zero_eid_p = jnp.zeros((n // tm_ffn,), I32)
    zero_eid_s = jnp.zeros((1,), I32)

    xp = x_prompt.reshape(n, d)
    xs = x_sample.reshape(nb, d)
    st_p, st_s = [], []
    for l in range(depth):
        wp = _prep_w_in(w_in[l])
        wq, wt = _prep_w_in_t(w_in[l])
        bfp = jnp.zeros((1, LANES), F32).at[0, _MISC_LF:_MISC_WI].set(b_forget[l].astype(F32))
        g_mix = norm_mix[l].reshape(1, d)
        g_x = norm_x[l].reshape(1, d)
        g_ffn = norm_ffn[l].reshape(1, d)
        wblk = _block_diag(w_pool[l])
        pscale = pool_scale[l].reshape(1, -1).astype(F32)
        wo16 = w_out[l].astype(BF16)
        wxq16 = w_xq[l].astype(BF16)
        wxo16 = w_xo[l].astype(BF16)

        (qa16, qb16, qi16, u, miscr, kat, vat, kbt, vbt, misct, ka16t, va16t, kb16t, vb16t, kir16t) = _proj_t(
            xp, g_mix, wq, wt, *tab_p, bfp.reshape(LANES, 1), b, t, ch)
        lft = misct[:, _MISC_LF:_MISC_WI, :]
        cft4 = jnp.cumsum(lft, axis=2).reshape(b, FOX_HEADS, nt, ch)
        oa = _fox_prompt(qa16, ka16t, va16t, cft4, b, t, ch)
        ob = _dsa_prompt(qi16, miscr, kir16t, qb16, kb16t, vb16t, b, t, tq_dsa, ch)
        oc = _pool_prompt(u, wblk, pscale, b, t, tt_pool)
        kvt, kvt16 = _norm_matmul_t(mem_prompt.reshape(b * n_mem, d), norm_mem[l].reshape(1, d),
                                    w_xkv[l].T.astype(BF16), b, n_mem)
        xp = _mixx_prompt(xp, oa, ob, oc, wo16, g_x, wxq16, kvt16[:, :xw], kvt16[:, xw:], wxo16, t, tm_mix)
        st_p.append((kat, vat, lft, kbt, vbt, misct[:, _MISC_KI:_MISC_KI + IDX_DIM, :],
                     u.reshape(b, t, -1)[:, -POOL_BUF:], kvt[:, :xw], kvt[:, xw:]))

        (ka_s, va_s, kb_s, vb_s, misc_s, u_s, qa16s, _, _, qb16s, _, _, qi16s) = _proj(
            xs, g_mix, wp, *tab_s, bfp, nb, 1)
        lf_s = misc_s[:, _MISC_LF:_MISC_WI]
        ki_s = misc_s[:, _MISC_KI:_MISC_KI + IDX_DIM]
        wi_s = misc_s[:, _MISC_WI:_MISC_WI + IDX_HEADS]
        lf_past = jnp.transpose(clogf[l][page_table], (0, 2, 1, 3)).reshape(nb, FOX_HEADS, n_past).astype(F32)
        csum = jnp.cumsum(lf_past, axis=2)
        fbias = lf_s[:, :, None] + (csum[:, :, -1:] - csum)
        fbias = jnp.transpose(fbias.reshape(nb, FOX_HEADS, n_pages, page), (0, 2, 1, 3))
        scores = _dsa_score_sample(page_table, qi16s.reshape(nb, IDX_HEADS, IDX_DIM),
                                   wi_s.reshape(nb, IDX_HEADS, 1), ki_s.reshape(nb, 1, IDX_DIM), ckidx, l)
        dmask = _dsa_select_sample(scores.reshape(nb, -1), n_past + 1,
                                   min(TOPK_MAX, (n_past + 1) // 4))
        oa_s, ob_s = _decode_attention(
            page_table, qa16s.reshape(nb, 1, fw), qb16s.reshape(nb, 1, dw),
            ka_s.reshape(nb, 1, fw), va_s.reshape(nb, 1, fw), kb_s.reshape(nb, 1, dw),
            vb_s.reshape(nb, 1, dw), fbias, dmask.reshape(nb, 1, -1), cfk, cfv, cdk, cdv, l)
        oc_s = _pool_sample(pool_t[l], u_s, wblk, pscale)
        xs = _mixx_sample(xs, oa_s.reshape(nb, fw), ob_s.reshape(nb, dw), oc_s, wo16, g_x, wxq16,
                          cmk, cmv, l, wxo16, 8)
        st_s.append((ka_s.reshape(nb, 1, FOX_HEADS, HEAD_DIM), va_s.reshape(nb, 1, FOX_HEADS, HEAD_DIM),
                     lf_s.reshape(nb, 1, FOX_HEADS), kb_s.reshape(nb, 1, DSA_HEADS, HEAD_DIM),
                     vb_s.reshape(nb, 1, DSA_HEADS, HEAD_DIM), ki_s.reshape(nb, 1, IDX_DIM),
                     jnp.concatenate([pool_t[l][1:], u_s[None]], axis=0)))

        i = l // 2
        if l % 2 == 0:
            wg16 = ffn_w_gate[i:i + 1].astype(BF16)
            wu16 = ffn_w_up[i:i + 1].astype(BF16)
            wd16 = ffn_w_down[i:i + 1].astype(BF16)
            xp = _ffn(zero_eid_p, xp, g_ffn, wg16, wu16, wd16, tm_ffn, tf_dense, dense=True)
            xs = _ffn(zero_eid_s, xs, g_ffn, wg16, wu16, wd16, nb, tf_dense, dense=True)
        else:
            wr16 = jnp.zeros((d, LANES), BF16).at[:, :N_EXPERTS].set(moe_router[i].astype(BF16))
            wg16 = moe_w_gate[i].astype(BF16)
            wu16 = moe_w_up[i].astype(BF16)
            wd16 = moe_w_down[i].astype(BF16)
            xp = _moe(xp, g_ffn, wr16, wg16, wu16, wd16, tm_moe, bm_moe, tf_moe)
            xs = _moe(xs, g_ffn, wr16, wg16, wu16, wd16, nb, 128, tf_moe)

    gf = norm_final.reshape(1, d)
    y_prompt = _rms_out(xp, gf, 1024).reshape(b, t, d)
    y_sample = _rms_out(xs, gf, nb).reshape(nb, 1, d)
    stack = lambda states, j: jnp.stack([s[j] for s in states], axis=0)

    def heads_last(a, heads):
        dp, bb, _, pp = a.shape
        return jnp.transpose(a.reshape(dp, bb, heads, HEAD_DIM, pp), (0, 1, 4, 2, 3))

    swap = lambda a: jnp.transpose(a, (0, 1, 3, 2))
    outs_p = (heads_last(stack(st_p, 0), FOX_HEADS), heads_last(stack(st_p, 1), FOX_HEADS),
              swap(stack(st_p, 2)), heads_last(stack(st_p, 3), DSA_HEADS),
              heads_last(stack(st_p, 4), DSA_HEADS), swap(stack(st_p, 5)), stack(st_p, 6),
              heads_last(stack(st_p, 7), X_HEADS), heads_last(stack(st_p, 8), X_HEADS))
    outs_s = tuple(stack(st_s, j) for j in range(6)) + (jnp.transpose(stack(st_s, 6), (0, 2, 1, 3)),)
    return (y_prompt, y_sample) + outs_p + outs_s
```

```python
import functools

import jax
import jax.numpy as jnp
import numpy as np
from jax import lax
from jax.experimental import pallas as pl
from jax.experimental.pallas import tpu as pltpu

F32 = jnp.float32
BF16 = jnp.bfloat16
I32 = jnp.int32

LANES = 128
HEAD_DIM = 64
FOX_HEADS = 8
DSA_HEADS = 4
IDX_HEADS = 8
IDX_DIM = 32
X_HEADS = 4
POOL_WINDOWS = (2, 4, 8, 16)
POOL_CH = 64
POOL_BUF = POOL_WINDOWS[-1] - 1
TOPK_MAX = 256
ROPE_THETA = 500000.0
N_EXPERTS = 8
EPS = 1e-6
NEG = -1e30
INT_MIN = -(2 ** 31)
VMEM_LIMIT = 56 * 1024 * 1024

_PW = (0, 512, 1024, 1536, 1792, 2048, 2304, 2560, 2688, 2944)
_MISC_KI, _MISC_LF, _MISC_WI = 0, 32, 40


def _cparams(sem):
    return pltpu.CompilerParams(dimension_semantics=sem, vmem_limit_bytes=VMEM_LIMIT)


def _dot(a, b):
    return jnp.dot(a, b, preferred_element_type=F32)


def _dot_nt(a, b):
    return lax.dot_general(a, b, (((1,), (1,)), ((), ())), preferred_element_type=F32)


def _rms(x, g):
    return x * lax.rsqrt(jnp.mean(x * x, axis=-1, keepdims=True) + EPS) * g


def _head_mask(shape, head, width, dtype):
    lane = lax.broadcasted_iota(I32, shape, len(shape) - 1)
    return jnp.where((lane >= head * width) & (lane < (head + 1) * width), 1.0, 0.0).astype(dtype)


def _rope(z, tab, half):
    n = z.shape[1]
    rep = n // LANES

    def wide(a):
        return a if rep == 1 else jnp.concatenate([a] * rep, axis=1)

    c = wide(tab[:, 0:LANES])
    sa = wide(tab[:, LANES:2 * LANES])
    sb = wide(tab[:, 2 * LANES:3 * LANES])
    return z * c + pltpu.roll(z, n - half, 1) * sa + pltpu.roll(z, half, 1) * sb


def _proj_kernel(x_ref, g_ref, w_ref, td_ref, ti_ref, tm_ref, bf_ref,
                 ka_ref, va_ref, kb_ref, vb_ref, misc_ref, u_ref,
                 qa16, ka16, va16, qb16, kb16, vb16, qi16):
    h = _rms(x_ref[...], g_ref[...]).astype(BF16)

    def mm(i):
        return _dot(h, w_ref[:, _PW[i]:_PW[i + 1]])

    scale = HEAD_DIM ** -0.5
    qa16[...] = (mm(0) * scale).astype(BF16)
    z = mm(1)
    ka_ref[...] = z
    ka16[...] = z.astype(BF16)
    z = mm(2)
    va_ref[...] = z
    va16[...] = z.astype(BF16)
    td = td_ref[...]
    qb16[...] = (_rope(mm(3), td, 8) * scale).astype(BF16)
    z = _rope(mm(4), td, 8)
    kb_ref[...] = z
    kb16[...] = z.astype(BF16)
    z = mm(5)
    vb_ref[...] = z
    vb16[...] = z.astype(BF16)
    qi16[...] = _rope(mm(6), ti_ref[...], 4).astype(BF16)
    z = mm(7)
    zr = _rope(z, tm_ref[...], 4)
    a = z + bf_ref[...]
    logsig = jnp.minimum(a, 0.0) - jnp.log(1.0 + jnp.exp(-jnp.abs(a)))
    lane = lax.broadcasted_iota(I32, z.shape, 1)
    misc_ref[...] = jnp.where((lane >= _MISC_LF) & (lane < _MISC_WI), logsig, zr)
    u_ref[...] = mm(8)


def _proj(x, g, wp, td, ti, tmi, bfp, tm, n_pos_tiles):
    n, d = x.shape
    grid = (n // tm,)
    row = lambda i: (i, 0)
    const = lambda i: (0, 0)
    tab = lambda i: (i % n_pos_tiles, 0)
    f32_w = (512, 512, 256, 256, 128, 256)
    b16_w = (512, 512, 512, 256, 256, 256, 256)
    out_shape = [jax.ShapeDtypeStruct((n, w), F32) for w in f32_w] + \
                [jax.ShapeDtypeStruct((n, w), BF16) for w in b16_w]
    out_specs = [pl.BlockSpec((tm, w), row) for w in f32_w + b16_w]
    return pl.pallas_call(
        _proj_kernel,
        grid=grid,
        in_specs=[pl.BlockSpec((tm, d), row), pl.BlockSpec((1, d), const),
                  pl.BlockSpec(wp.shape, const),
                  pl.BlockSpec((tm, 3 * LANES), tab), pl.BlockSpec((tm, 3 * LANES), tab),
                  pl.BlockSpec((tm, 3 * LANES), tab), pl.BlockSpec((1, LANES), const)],
        out_specs=out_specs,
        out_shape=out_shape,
        compiler_params=_cparams(("arbitrary",)),
        name="proj_in",
    )(x, g, wp, td, ti, tmi, bfp)


def _rope_table(pos, head, rot, active):
    half = rot // 2
    inv = jnp.float32(ROPE_THETA) ** (-jnp.arange(half, dtype=jnp.float32) / half)
    ang = pos.astype(jnp.float32)[:, None] * inv[None, :]
    cos, sin = jnp.cos(ang), jnp.sin(ang)
    lane = np.arange(LANES)
    jj = lane % head
    idx = jj % half
    on = lane < active
    c = jnp.where((jj < rot) & on, cos[:, idx], 1.0)
    sa = jnp.where((jj < half) & on, -sin[:, idx], 0.0)
    sb = jnp.where((jj >= half) & (jj < rot) & on, sin[:, idx], 0.0)
    return jnp.concatenate([c, sa, sb], axis=1).astype(F32)


def _rope_t(z, tab, half, period):
    f = z.shape[0]
    rep = f // period

    def tall(a):
        return a if rep == 1 else jnp.concatenate([a] * rep, axis=0)

    c = tall(tab[0:period])
    sa = tall(tab[period:2 * period])
    sb = tall(tab[2 * period:3 * period])
    up = jnp.concatenate([z[half:], z[:half]], axis=0)
    dn = jnp.concatenate([z[f - half:], z[:f - half]], axis=0)
    return z * c + up * sa + dn * sb


def _log_sigmoid(a):
    return jnp.minimum(a, 0.0) - jnp.log(1.0 + jnp.exp(-jnp.abs(a)))


_QW = (0, 512, 768, 1024, 1280, 1536, 1792)
_TW = (0, 512, 1024, 1280, 1536, 1664)


def _proj_t_kernel(x_ref, g_ref, wq_ref, wt_ref, td_ref, ti_ref, tdt_ref, tit_ref, bfc_ref,
                   qa16, qb16, qi16, u_ref, kb16, kir16,
                   kat, vat, kbt, vbt, misct, ka16t, va16t, vb16t):
    h = _rms(x_ref[...], g_ref[...]).astype(BF16)

    def mq(i):
        return _dot(h, wq_ref[:, _QW[i]:_QW[i + 1]])

    def mt(i):
        return _dot_nt(wt_ref[_TW[i]:_TW[i + 1], :], h)

    scale = HEAD_DIM ** -0.5
    td = td_ref[...]
    ti = ti_ref[...]
    qa16[...] = (mq(0) * scale).astype(BF16)
    qb16[...] = (_rope(mq(1), td, 8) * scale).astype(BF16)
    qi16[...] = _rope(mq(2), ti, 4).astype(BF16)
    u_ref[...] = mq(3)
    kb16[...] = _rope(mq(4), td, 8).astype(BF16)
    kir16[...] = _rope(mq(5), ti, 4).astype(BF16)
    z = mt(0)
    kat[0] = z
    ka16t[0, 0] = z.astype(BF16)
    z = mt(1)
    vat[0] = z
    va16t[0, 0] = z.astype(BF16)
    kbt[0] = _rope_t(mt(2), tdt_ref[...], 8, HEAD_DIM)
    z = mt(3)
    vbt[0] = z
    vb16t[0, 0] = z.astype(BF16)
    z = mt(4)
    ki = _rope_t(z[0:IDX_DIM], tit_ref[...], 4, IDX_DIM)
    row = lax.broadcasted_iota(I32, z.shape, 0)
    rest = jnp.where((row >= _MISC_LF) & (row < _MISC_WI), _log_sigmoid(z + bfc_ref[...]), z)
    misct[0] = jnp.concatenate([ki, rest[IDX_DIM:]], axis=0)


def _proj_t(x, g, wq, wt, td, ti, tdt, tit, bfc, b, t, tm):
    n, d = x.shape
    nt = t // tm
    row = lambda bi, ti_: (bi * nt + ti_, 0)
    const = lambda bi, ti_: (0, 0)
    tab = lambda bi, ti_: (ti_, 0)
    tabt = lambda bi, ti_: (0, ti_)
    fm = lambda bi, ti_: (bi, 0, ti_)
    ch = lambda bi, ti_: (bi, ti_, 0, 0)
    row_out = [(512, BF16), (256, BF16), (256, BF16), (256, F32), (256, BF16), (256, BF16)]
    fm_out = [512, 512, 256, 256, 128]
    ch_out = [512, 512, 256]
    out_shape = ([jax.ShapeDtypeStruct((n, w), dt) for w, dt in row_out]
                 + [jax.ShapeDtypeStruct((b, w, t), F32) for w in fm_out]
                 + [jax.ShapeDtypeStruct((b, nt, w, tm), BF16) for w in ch_out])
    out_specs = ([pl.BlockSpec((tm, w), row) for w, _ in row_out]
                 + [pl.BlockSpec((1, w, tm), fm) for w in fm_out]
                 + [pl.BlockSpec((1, 1, w, tm), ch) for w in ch_out])
    return pl.pallas_call(
        _proj_t_kernel,
        grid=(b, nt),
        in_specs=[pl.BlockSpec((tm, d), row), pl.BlockSpec((1, d), const),
                  pl.BlockSpec(wq.shape, const), pl.BlockSpec(wt.shape, const),
                  pl.BlockSpec((tm, 3 * LANES), tab), pl.BlockSpec((tm, 3 * LANES), tab),
                  pl.BlockSpec((3 * HEAD_DIM, tm), tabt), pl.BlockSpec((3 * IDX_DIM, tm), tabt),
                  pl.BlockSpec((LANES, 1), const)],
        out_specs=out_specs,
        out_shape=out_shape,
        compiler_params=_cparams(("arbitrary", "arbitrary")),
        name="proj_in_prompt",
    )(x, g, wq, wt, td, ti, tdt, tit, bfc)


def _rope_table_t(pos, head, rot):
    half = rot // 2
    inv = jnp.float32(ROPE_THETA) ** (-jnp.arange(half, dtype=jnp.float32) / half)
    ang = pos.astype(jnp.float32)[:, None] * inv[None, :]
    cos, sin = jnp.cos(ang).T, jnp.sin(ang).T
    jj = np.arange(head)
    idx = jj % half
    c = jnp.where((jj < rot)[:, None], cos[idx], 1.0)
    sa = jnp.where((jj < half)[:, None], -sin[idx], 0.0)
    sb = jnp.where(((jj >= half) & (jj < rot))[:, None], sin[idx], 0.0)
    return jnp.concatenate([c, sa, sb], axis=0).astype(F32)


def _nmm_t_kernel(x_ref, g_ref, wt_ref, o_ref, o16_ref):
    z = _dot_nt(wt_ref[...], _rms(x_ref[...], g_ref[...]).astype(BF16))
    o_ref[0] = z
    o16_ref[0] = z.astype(BF16)


def _norm_matmul_t(x, g, wt16, nblk, tm):
    d = x.shape[1]
    m = wt16.shape[0]
    return pl.pallas_call(
        _nmm_t_kernel,
        grid=(nblk,),
        in_specs=[pl.BlockSpec((tm, d), lambda i: (i, 0)), pl.BlockSpec((1, d), lambda i: (0, 0)),
                  pl.BlockSpec((m, d), lambda i: (0, 0))],
        out_specs=[pl.BlockSpec((1, m, tm), lambda i: (i, 0, 0)), pl.BlockSpec((1, m, tm), lambda i: (i, 0, 0))],
        out_shape=[jax.ShapeDtypeStruct((nblk, m, tm), F32), jax.ShapeDtypeStruct((nblk, m, tm), BF16)],
        compiler_params=_cparams(("arbitrary",)),
        name="norm_matmul_t",
    )(x, g, wt16)


def _fox_kernel(q_ref, kt_ref, vt_ref, cft_ref, o_ref, *, tq, ch):
    hp = pl.program_id(1)
    qt = pl.program_id(2)
    q = q_ref[...]
    qms = [q * _head_mask((1, LANES), i, HEAD_DIM, BF16) for i in range(2)]
    per = ch // tq
    kdiag = qt // per
    row = (qt % per) * tq + lax.broadcasted_iota(I32, (tq, ch), 0)
    causal = lax.broadcasted_iota(I32, (tq, ch), 1) <= row

    def step(kt, carry, masked):
        kk = kt_ref[0, kt]
        vv = vt_ref[0, kt]
        out = []
        for i in range(2):
            m, l, acc = carry[3 * i:3 * i + 3]
            ck = cft_ref[0, pl.ds(2 * hp + i, 1), pl.ds(kt, 1), :].reshape(1, ch)
            s = _dot(qms[i], kk) - ck
            if masked:
                s = jnp.where(causal, s, NEG)
            m_new = jnp.maximum(m, jnp.max(s, axis=1, keepdims=True))
            alpha = jnp.exp(m - m_new)
            p = jnp.exp(s - m_new)
            l = alpha * l + jnp.sum(p, axis=1, keepdims=True)
            acc = alpha * acc + _dot_nt(p.astype(BF16), vv)
            out += [m_new, l, acc]
        return tuple(out)

    init = (jnp.full((tq, 1), NEG, F32), jnp.zeros((tq, 1), F32), jnp.zeros((tq, LANES), F32)) * 2
    carry = lax.fori_loop(0, kdiag, functools.partial(step, masked=False), init)
    carry = step(kdiag, carry, True)
    lane = lax.broadcasted_iota(I32, (tq, LANES), 1)
    o_ref[...] = jnp.where(lane < HEAD_DIM, carry[2] / carry[1], carry[5] / carry[4]).astype(BF16)


def _fox_prompt(qa16, ka16t, va16t, cft4, b, t, tq, ch):
    nq = t // tq
    nc = t // ch
    n = b * t
    kv = pl.BlockSpec((1, nc, LANES, ch), lambda bi, hp, qi: (bi, 0, hp, 0))
    return pl.pallas_call(
        functools.partial(_fox_kernel, tq=tq, ch=ch),
        grid=(b, FOX_HEADS // 2, nq),
        in_specs=[pl.BlockSpec((tq, LANES), lambda bi, hp, qi: (bi * nq + qi, hp)), kv, kv,
                  pl.BlockSpec((1, FOX_HEADS, nc, ch), lambda bi, hp, qi: (bi, 0, 0, 0))],
        out_specs=pl.BlockSpec((tq, LANES), lambda bi, hp, qi: (bi * nq + qi, hp)),
        out_shape=jax.ShapeDtypeStruct((n, FOX_HEADS * HEAD_DIM), BF16),
        compiler_params=_cparams(("arbitrary", "arbitrary", "arbitrary")),
        name="fox_prompt",
    )(qa16, ka16t, va16t, cft4)


def _fold_rows(x, op, rows=8):
    while x.shape[0] > rows:
        half = x.shape[0] // 2
        x = op(x[:half], x[half:])
    return x


def _key_to_f32(key):
    bits = key ^ (lax.shift_right_arithmetic(key, 31) & 0x7FFFFFFF)
    return lax.bitcast_convert_type(bits, F32)


def _select_threshold(sc_ref, nch, nkeys, nq, ksel):
    kf = float(ksel)
    lowest = float(np.finfo(np.float32).min)

    def count_ge(cand):
        def body(c, acc):
            return acc + _fold_rows(jnp.where(sc_ref[c] >= cand, 1.0, 0.0), jnp.add)

        acc = lax.fori_loop(0, nch, body, jnp.zeros((8, nq), F32))
        return jnp.sum(acc, axis=0, keepdims=True)

    key = jnp.where(count_ge(jnp.zeros((1, nq), F32)) >= kf, 0, INT_MIN).astype(I32)

    def bit_body(bi, key):
        cand = key | lax.shift_left(jnp.int32(1), 30 - bi)
        return jnp.where(count_ge(_key_to_f32(cand)) >= kf, cand, key)

    key = lax.fori_loop(0, 31, bit_body, key)
    tau = _key_to_f32(key)
    real = tau >= lowest
    tie = real & (count_ge(tau) > kf)

    @pl.when(jnp.max(jnp.where(tie, 1.0, 0.0)) > 0.0)
    def _():
        need = kf - count_ge(_key_to_f32(key + 1))
        r_ = lax.broadcasted_iota(I32, (LANES, LANES), 0)
        c_ = lax.broadcasted_iota(I32, (LANES, LANES), 1)
        tri = jnp.where(c_ < r_, 1.0, 0.0).astype(BF16)

        def body(c, carry):
            for s in range(nkeys // LANES):
                kk = sc_ref[c, s * LANES:(s + 1) * LANES, :]
                eq = kk == tau
                eqf = jnp.where(eq, 1.0, 0.0)
                pre = _dot(tri, eqf.astype(BF16)) + carry
                drop = eq & (pre >= need) & tie
                sc_ref[c, s * LANES:(s + 1) * LANES, :] = jnp.where(drop, -jnp.inf, kk)
                carry = carry + jnp.sum(eqf, axis=0, keepdims=True)
            return carry

        lax.fori_loop(0, nch, body, jnp.zeros((1, nq), F32))

    return jnp.where(real, tau, lowest)


def _dsa_kernel(qi_ref, mt_ref, kir_ref, q_ref, k_ref, vt_ref, o_ref, sc_ref, *, tq, w, n_sel):
    qt = pl.program_id(1)
    nch = ((qt + 1) * tq + w - 1) // w
    qi = qi_ref[...]
    mt = mt_ref[0]
    kpos = lax.broadcasted_iota(I32, (w, tq), 0)
    qpos = qt * tq + lax.broadcasted_iota(I32, (w, tq), 1)
    qms = [qi * _head_mask((1, IDX_HEADS * IDX_DIM), j, IDX_DIM, BF16) for j in range(IDX_HEADS)]
    wrows = [mt[_MISC_WI + j:_MISC_WI + j + 1, :] for j in range(IDX_HEADS)]

    def score_chunk(c, carry):
        kk = kir_ref[pl.ds(pl.multiple_of(c * w, w), w), :]
        sc = jnp.zeros((w, tq), F32)
        for j in range(IDX_HEADS):
            sc = sc + wrows[j] * jnp.maximum(_dot_nt(kk, qms[j]), 0.0)
        sc_ref[c] = jnp.where(c * w + kpos <= qpos, sc, -jnp.inf)
        return carry

    lax.fori_loop(0, nch, score_chunk, 0)
    tau = _select_threshold(sc_ref, nch, w, tq, n_sel)

    q = q_ref[...]
    nl = DSA_HEADS * HEAD_DIM
    qhs = [q * _head_mask((1, nl), h, HEAD_DIM, BF16) for h in range(DSA_HEADS)]

    def step(c, carry):
        kk = k_ref[pl.ds(pl.multiple_of(c * w, w), w), :]
        vv = vt_ref[0, c]
        sel = sc_ref[c] >= tau
        out = []
        for h in range(DSA_HEADS):
            m, l, acc = carry[3 * h:3 * h + 3]
            s = jnp.where(sel, _dot_nt(kk, qhs[h]), NEG)
            m_new = jnp.maximum(m, jnp.max(_fold_rows(s, jnp.maximum), axis=0, keepdims=True))
            alpha = jnp.exp(m - m_new)
            p = jnp.exp(s - m_new)
            l = alpha * l + jnp.sum(_fold_rows(p, jnp.add), axis=0, keepdims=True)
            acc = alpha * acc + _dot(vv, p.astype(BF16))
            out += [m_new, l, acc]
        return tuple(out)

    init = (jnp.full((1, tq), NEG, F32), jnp.zeros((1, tq), F32), jnp.zeros((nl, tq), F32)) * DSA_HEADS
    carry = lax.fori_loop(0, nch, step, init)
    frow = lax.broadcasted_iota(I32, (nl, tq), 0)
    out_t = jnp.zeros((nl, tq), F32)
    for h in range(DSA_HEADS):
        keep = (frow >= h * HEAD_DIM) & (frow < (h + 1) * HEAD_DIM)
        out_t = jnp.where(keep, carry[3 * h + 2] / carry[3 * h + 1], out_t)
    o_ref[...] = out_t.T.astype(BF16)


def _dsa_prompt(qi16, misct, kir16, qb16, kb16, vb16t, b, t, tq, w):
    nq = t // tq
    n = b * t
    nl = DSA_HEADS * HEAD_DIM
    n_sel = min(TOPK_MAX, t // 4)
    qrow = lambda bi, qi: (bi * nq + qi, 0)
    keys = pl.BlockSpec((t, nl), lambda bi, qi: (bi, 0))
    return pl.pallas_call(
        functools.partial(_dsa_kernel, tq=tq, w=w, n_sel=n_sel),
        grid=(b, nq),
        in_specs=[pl.BlockSpec((tq, nl), qrow), pl.BlockSpec((1, LANES, tq), lambda bi, qi: (bi, 0, qi)),
                  keys, pl.BlockSpec((tq, nl), qrow), keys,
                  pl.BlockSpec((1, t // w, nl, w), lambda bi, qi: (bi, 0, 0, 0))],
        out_specs=pl.BlockSpec((tq, nl), qrow),
        out_shape=jax.ShapeDtypeStruct((n, nl), BF16),
        scratch_shapes=[pltpu.VMEM((t // w, w, tq), F32)],
        compiler_params=_cparams(("arbitrary", "arbitrary")),
        name="dsa_prompt",
    )(qi16, misct, kir16, qb16, kb16, vb16t)


def _pool_mix(win_sums, x, cnts, wblk, scale):
    lane = lax.broadcasted_iota(I32, x.shape, 1)
    mean = jnp.zeros_like(x)
    for g, wdw in enumerate(POOL_WINDOWS):
        mg = win_sums[wdw] / cnts[g]
        mean = jnp.where((lane >= g * POOL_CH) & (lane < (g + 1) * POOL_CH), mg, mean)
    return (_dot((mean - x).astype(BF16), wblk) * scale).astype(BF16)


def _pool_kernel(u_ref, wblk_ref, scale_ref, o_ref, ext_ref, *, tt):
    t = pl.program_id(1)
    hist = POOL_WINDOWS[-1]

    @pl.when(t == 0)
    def _():
        ext_ref[0:hist, :] = jnp.zeros((hist, ext_ref.shape[1]), F32)

    x = u_ref[...]
    ext_ref[hist:hist + tt, :] = x
    acc = x
    sums = {}
    for s in range(1, hist):
        acc = acc + ext_ref[hist - s:hist - s + tt, :]
        if s + 1 in POOL_WINDOWS:
            sums[s + 1] = acc
    pos = t * tt + lax.broadcasted_iota(I32, (tt, 1), 0)
    cnts = [jnp.minimum(pos + 1, wdw).astype(F32) for wdw in POOL_WINDOWS]
    o_ref[...] = _pool_mix(sums, x, cnts, wblk_ref[...], scale_ref[...])
    ext_ref[0:hist, :] = ext_ref[tt:tt + hist, :]


def _pool_prompt(u, wblk16, scale, b, t, tt):
    n, c = u.shape
    nt = t // tt
    return pl.pallas_call(
        functools.partial(_pool_kernel, tt=tt),
        grid=(b, nt),
        in_specs=[pl.BlockSpec((tt, c), lambda bi, ti: (bi * nt + ti, 0)),
                  pl.BlockSpec((c, c), lambda bi, ti: (0, 0)),
                  pl.BlockSpec((1, c), lambda bi, ti: (0, 0))],
        out_specs=pl.BlockSpec((tt, c), lambda bi, ti: (bi * nt + ti, 0)),
        out_shape=jax.ShapeDtypeStruct((n, c), BF16),
        scratch_shapes=[pltpu.VMEM((tt + POOL_WINDOWS[-1], c), F32)],
        compiler_params=_cparams(("arbitrary", "arbitrary")),
        name="pool_prompt",
    )(u, wblk16, scale)


def _pool_s_kernel(st_ref, u_ref, wblk_ref, scale_ref, o_ref):
    x = u_ref[...]
    acc = x
    sums = {}
    for s in range(1, POOL_WINDOWS[-1]):
        acc = acc + st_ref[POOL_BUF - s]
        if s + 1 in POOL_WINDOWS:
            sums[s + 1] = acc
    cnts = [float(wdw) for wdw in POOL_WINDOWS]
    o_ref[...] = _pool_mix(sums, x, cnts, wblk_ref[...], scale_ref[...])


def _pool_sample(state_t, u, wblk16, scale):
    n, c = u.shape
    return pl.pallas_call(
        _pool_s_kernel,
        out_shape=jax.ShapeDtypeStruct((n, c), BF16),
        name="pool_sample",
    )(state_t, u, wblk16, scale)


def _mix_out(x, oa, ob, oc, wo_ref):
    na, nb = oa.shape[1], ob.shape[1]
    return (x + _dot(oa, wo_ref[0:na, :]) + _dot(ob, wo_ref[na:na + nb, :])
            + _dot(oc, wo_ref[na + nb:, :]))


def _mixx_kernel(x_ref, oa_ref, ob_ref, oc_ref, wo_ref, gx_ref, wxq_ref, mk_ref, mv_ref, wxo_ref,
                 o_ref):
    x1 = _mix_out(x_ref[...], oa_ref[...], ob_ref[...], oc_ref[...], wo_ref)
    h = _rms(x1, gx_ref[...]).astype(BF16)
    q = (_dot(h, wxq_ref[...]) * HEAD_DIM ** -0.5).astype(BF16)
    mk = mk_ref[0]
    mv = mv_ref[0]
    nl = X_HEADS * HEAD_DIM
    o = jnp.zeros(q.shape, F32)
    for hh in range(X_HEADS):
        hm = _head_mask((1, nl), hh, HEAD_DIM, F32)
        s = _dot(q * hm.astype(BF16), mk)
        p = jnp.exp(s - jnp.max(s, axis=1, keepdims=True))
        l = jnp.sum(p, axis=1, keepdims=True)
        o = o + (_dot_nt(p.astype(BF16), mv) / l) * hm
    o_ref[...] = x1 + _dot(o.astype(BF16), wxo_ref[...])


def _mixx_prompt(x, oa, ob, oc, wo16, gx, wxq16, mk16, mv16, wxo16, t, tm):
    n, d = x.shape
    per_b = t // tm
    row = lambda i: (i, 0)
    const = lambda i: (0, 0)
    mem = lambda i: (i // per_b, 0, 0)
    return pl.pallas_call(
        _mixx_kernel,
        grid=(n // tm,),
        in_specs=[pl.BlockSpec((tm, d), row), pl.BlockSpec((tm, oa.shape[1]), row),
                  pl.BlockSpec((tm, ob.shape[1]), row), pl.BlockSpec((tm, oc.shape[1]), row),
                  pl.BlockSpec(wo16.shape, const), pl.BlockSpec((1, d), const),
                  pl.BlockSpec(wxq16.shape, const),
                  pl.BlockSpec((1,) + mk16.shape[1:], mem), pl.BlockSpec((1,) + mv16.shape[1:], mem),
                  pl.BlockSpec(wxo16.shape, const)],
        out_specs=pl.BlockSpec((tm, d), row),
        out_shape=jax.ShapeDtypeStruct((n, d), F32),
        compiler_params=_cparams(("arbitrary",)),
        name="mix_xattn_prompt",
    )(x, oa, ob, oc, wo16, gx, wxq16, mk16, mv16, wxo16)


def _rows_by_head(v, nrows, width):
    lanes = v.shape[1]
    r = lax.broadcasted_iota(I32, (nrows, lanes), 0)
    lane = lax.broadcasted_iota(I32, (nrows, lanes), 1)
    keep = (lane >= r * width) & (lane < (r + 1) * width)
    return jnp.where(keep, jnp.broadcast_to(v, (nrows, lanes)), 0.0), keep


def _mixx_s_kernel(x_ref, oa_ref, ob_ref, oc_ref, wo_ref, gx_ref, wxq_ref, mk_ref, mv_ref, wxo_ref,
                   o_ref, att_ref, *, sb):
    x1 = _mix_out(x_ref[...], oa_ref[...], ob_ref[...], oc_ref[...], wo_ref)
    h = _rms(x1, gx_ref[...]).astype(BF16)
    q = _dot(h, wxq_ref[...]) * HEAD_DIM ** -0.5
    for b in range(sb):
        qrows, keep = _rows_by_head(q[b:b + 1, :], 8, HEAD_DIM)
        s = _dot(qrows.astype(BF16), mk_ref[0, b].astype(BF16))
        p = jnp.exp(s - jnp.max(s, axis=1, keepdims=True))
        l = jnp.sum(p, axis=1, keepdims=True)
        ob_ = _dot_nt(p.astype(BF16), mv_ref[0, b].astype(BF16)) / l
        att_ref[b:b + 1, :] = jnp.sum(jnp.where(keep, ob_, 0.0), axis=0, keepdims=True)
    o_ref[...] = x1 + _dot(att_ref[...].astype(BF16), wxo_ref[...])


def _mixx_sample(x, oa, ob, oc, wo16, gx, wxq16, memk, memv, layer, wxo16, sb):
    n, d = x.shape
    nl = X_HEADS * HEAD_DIM
    row = lambda i: (i, 0)
    const = lambda i: (0, 0)
    mem = lambda i: (layer, i, 0, 0)
    mblk = (1, sb) + memk.shape[2:]
    return pl.pallas_call(
        functools.partial(_mixx_s_kernel, sb=sb),
        grid=(n // sb,),
        in_specs=[pl.BlockSpec((sb, d), row), pl.BlockSpec((sb, oa.shape[1]), row),
                  pl.BlockSpec((sb, ob.shape[1]), row), pl.BlockSpec((sb, oc.shape[1]), row),
                  pl.BlockSpec(wo16.shape, const), pl.BlockSpec((1, d), const),
                  pl.BlockSpec(wxq16.shape, const),
                  pl.BlockSpec(mblk, mem), pl.BlockSpec(mblk, mem),
                  pl.BlockSpec(wxo16.shape, const)],
        out_specs=pl.BlockSpec((sb, d), row),
        out_shape=jax.ShapeDtypeStruct((n, d), F32),
        scratch_shapes=[pltpu.VMEM((sb, nl), F32)],
        compiler_params=_cparams(("arbitrary",)),
        name="mix_xattn_sample",
    )(x, oa, ob, oc, wo16, gx, wxq16, memk, memv, wxo16)


def _swiglu_step(h, wg, wu, wd):
    a = _dot(h, wg)
    b = _dot(h, wu)
    t = (a * (1.0 / (1.0 + jnp.exp(-a)))) * b
    return _dot(t.astype(BF16), wd)


def _ffn_kernel(x_ref, g_ref, wg_ref, wu_ref, wd_ref, o_ref, h_ref, acc_ref):
    j = pl.program_id(1)

    @pl.when(j == 0)
    def _():
        h_ref[...] = _rms(x_ref[...], g_ref[...]).astype(BF16)
        acc_ref[...] = jnp.zeros(acc_ref.shape, F32)

    acc_ref[...] += _swiglu_step(h_ref[...], wg_ref[0], wu_ref[0], wd_ref[0])

    @pl.when(j == pl.num_programs(1) - 1)
    def _():
        o_ref[...] = x_ref[...] + acc_ref[...]


def _ffn_dense(x, g, wg16, wu16, wd16, bm, tf):
    n, d = x.shape
    dff = wg16.shape[2]
    return pl.pallas_call(
        _ffn_kernel,
        grid=(n // bm, dff // tf),
        in_specs=[pl.BlockSpec((bm, d), lambda i, j: (i, 0)),
                  pl.BlockSpec((1, d), lambda i, j: (0, 0)),
                  pl.BlockSpec((1, d, tf), lambda i, j: (0, 0, j)),
                  pl.BlockSpec((1, d, tf), lambda i, j: (0, 0, j)),
                  pl.BlockSpec((1, tf, d), lambda i, j: (0, j, 0))],
        out_specs=pl.BlockSpec((bm, d), lambda i, j: (i, 0)),
        out_shape=jax.ShapeDtypeStruct((n, d), F32),
        scratch_shapes=[pltpu.VMEM((bm, d), BF16), pltpu.VMEM((bm, d), F32)],
        compiler_params=_cparams(("arbitrary", "arbitrary")),
        name="ffn_dense",
    )(x, g, wg16, wu16, wd16)


def _start_row_gather(src_hbm, idx_ref, dst_ref, sem, n_rows):
    def body(r, carry):
        pltpu.make_async_copy(src_hbm.at[pl.ds(idx_ref[0, 0, r], 1)], dst_ref.at[pl.ds(r, 1)], sem).start()
        return carry

    lax.fori_loop(0, n_rows, body, 0, unroll=8)


def _wait_row_gather(src_hbm, dst_ref, sem, n_rows):
    pltpu.make_async_copy(src_hbm.at[pl.ds(0, n_rows)], dst_ref, sem).wait()


def _experts_kernel(eid_ref, nu_ref, idx_ref, idxn_ref, h_hbm, wg_ref, wu_ref, wd_ref, o_ref,
                    xs_ref, sem, h_ref, acc_ref, *, bm, rp, nf):
    i = pl.program_id(0)
    j = pl.program_id(1)
    slot = i % 2
    n_used = nu_ref[0]
    used = i < n_used
    per = rp // nf

    @pl.when((i == 0) & (j == 0))
    def _():
        _start_row_gather(h_hbm, idx_ref, xs_ref.at[0], sem.at[0], rp)

    @pl.when(used & (j == 0))
    def _():
        _wait_row_gather(h_hbm, xs_ref.at[slot], sem.at[slot], rp)
        h_ref[...] = xs_ref[slot, 0:bm].astype(BF16)
        acc_ref[...] = jnp.zeros(acc_ref.shape, F32)

    @pl.when(used)
    def _():
        base = j * per
        for r in range(per):
            pltpu.make_async_copy(h_hbm.at[pl.ds(idxn_ref[0, 0, base + r], 1)],
                                  xs_ref.at[1 - slot, pl.ds(base + r, 1)], sem.at[1 - slot]).start()
        acc_ref[...] += _swiglu_step(h_ref[...], wg_ref[0].astype(BF16), wu_ref[0].astype(BF16),
                                     wd_ref[0].astype(BF16))

    @pl.when(j == nf - 1)
    def _():
        o_ref[...] = jnp.where(used, acc_ref[...], 0.0)

    @pl.when((i == pl.num_programs(0) - 1) & (j == nf - 1))
    def _():
        _wait_row_gather(h_hbm, xs_ref.at[n_used % 2], sem.at[n_used % 2], rp)


def _ffn_experts(blk_e, n_used, row_tok, h32, wg, wu, wd, bm, tf):
    d = h32.shape[1]
    nblk = blk_e.shape[0]
    nf = wg.shape[2] // tf
    per = -(-bm // nf)
    per = -(-per // 8) * 8
    rp = per * nf
    idx3 = jnp.pad(row_tok.reshape(nblk, bm), ((0, 0), (0, rp - bm))).reshape(nblk, 1, rp)
    fj = lambda i, j, nu: jnp.where(i < nu[0], j, nf - 1)
    gs = pltpu.PrefetchScalarGridSpec(
        num_scalar_prefetch=2,
        grid=(nblk, nf),
        in_specs=[pl.BlockSpec((1, 1, rp), lambda i, j, e, nu: (i, 0, 0), memory_space=pltpu.SMEM),
                  pl.BlockSpec((1, 1, rp), lambda i, j, e, nu: (jnp.minimum(i + 1, nu[0] - 1), 0, 0),
                               memory_space=pltpu.SMEM),
                  pl.BlockSpec(memory_space=pl.ANY),
                  pl.BlockSpec((1, d, tf), lambda i, j, e, nu: (e[i], 0, fj(i, j, nu))),
                  pl.BlockSpec((1, d, tf), lambda i, j, e, nu: (e[i], 0, fj(i, j, nu))),
                  pl.BlockSpec((1, tf, d), lambda i, j, e, nu: (e[i], fj(i, j, nu), 0))],
        out_specs=pl.BlockSpec((bm, d), lambda i, j, e, nu: (i, 0)),
        scratch_shapes=[pltpu.VMEM((2, rp, d), F32), pltpu.SemaphoreType.DMA((2,)),
                        pltpu.VMEM((bm, d), BF16), pltpu.VMEM((bm, d), F32)],
    )
    return pl.pallas_call(
        functools.partial(_experts_kernel, bm=bm, rp=rp, nf=nf),
        grid_spec=gs,
        out_shape=jax.ShapeDtypeStruct((nblk * bm, d), F32),
        compiler_params=_cparams(("arbitrary", "arbitrary")),
        name="ffn_experts",
    )(blk_e, n_used, idx3, idx3, h32, wg, wu, wd)


def _combine_kernel(idx_ref, idxn_ref, x_ref, info_ref, g_ref, ys_hbm, o_ref, buf_ref, sem, *, tm, final):
    i = pl.program_id(0)
    slot = i % 2

    @pl.when(i == 0)
    def _():
        _start_row_gather(ys_hbm, idx_ref, buf_ref.at[0], sem.at[0], 2 * tm)

    _wait_row_gather(ys_hbm, buf_ref.at[slot], sem.at[slot], 2 * tm)

    @pl.when(i + 1 < pl.num_programs(0))
    def _():
        _start_row_gather(ys_hbm, idxn_ref, buf_ref.at[1 - slot], sem.at[1 - slot], 2 * tm)

    info = info_ref[...]
    y = x_ref[...] + info[:, 2:3] * buf_ref[slot, 0:tm] + info[:, 3:4] * buf_ref[slot, tm:2 * tm]
    o_ref[...] = _rms(y, g_ref[...]) if final else y


def _moe_combine(x, info, gf, ys, dest2, tm, final):
    n, d = x.shape
    nt = n // tm
    idx3 = jnp.transpose(dest2.reshape(nt, tm, 2), (0, 2, 1)).reshape(nt, 1, 2 * tm)
    return pl.pallas_call(
        functools.partial(_combine_kernel, tm=tm, final=final),
        grid=(nt,),
        in_specs=[pl.BlockSpec((1, 1, 2 * tm), lambda i: (i, 0, 0), memory_space=pltpu.SMEM),
                  pl.BlockSpec((1, 1, 2 * tm), lambda i: (jnp.minimum(i + 1, nt - 1), 0, 0),
                               memory_space=pltpu.SMEM),
                  pl.BlockSpec((tm, d), lambda i: (i, 0)), pl.BlockSpec((tm, LANES), lambda i: (i, 0)),
                  pl.BlockSpec((1, d), lambda i: (0, 0)), pl.BlockSpec(memory_space=pl.ANY)],
        out_specs=pl.BlockSpec((tm, d), lambda i: (i, 0)),
        out_shape=jax.ShapeDtypeStruct((n, d), F32),
        scratch_shapes=[pltpu.VMEM((2, 2 * tm, d), F32), pltpu.SemaphoreType.DMA((2,))],
        compiler_params=_cparams(("arbitrary",)),
        name="moe_combine",
    )(idx3, idx3, x, info, gf, ys)


def _router_kernel(x_ref, g_ref, wr_ref, h_ref, info_ref):
    h = _rms(x_ref[...], g_ref[...])
    h_ref[...] = h
    w = wr_ref[...]
    h_hi = h.astype(BF16)
    h_lo = (h - h_hi.astype(F32)).astype(BF16)
    w_hi = w.astype(BF16)
    w_lo = (w - w_hi.astype(F32)).astype(BF16)
    logits = _dot(h_hi, w_hi) + (_dot(h_hi, w_lo) + _dot(h_lo, w_hi))
    lane = lax.broadcasted_iota(I32, logits.shape, 1)
    logits = jnp.where(lane < N_EXPERTS, logits, NEG)
    m1 = jnp.max(logits, axis=1, keepdims=True)
    i1 = jnp.min(jnp.where(logits == m1, lane, LANES), axis=1, keepdims=True)
    rest = jnp.where(lane == i1, NEG, logits)
    m2 = jnp.max(rest, axis=1, keepdims=True)
    i2 = jnp.min(jnp.where(rest == m2, lane, LANES), axis=1, keepdims=True)
    e = jnp.exp(m2 - m1)
    g1 = 1.0 / (1.0 + e)
    g2 = e / (1.0 + e)
    info = jnp.where(lane == 0, i1.astype(F32), jnp.where(lane == 1, i2.astype(F32),
                     jnp.where(lane == 2, g1, jnp.where(lane == 3, g2, 0.0))))
    info_ref[...] = info


def _router(x, g, wr16, tm):
    n, d = x.shape
    return pl.pallas_call(
        _router_kernel,
        grid=(n // tm,),
        in_specs=[pl.BlockSpec((tm, d), lambda i: (i, 0)), pl.BlockSpec((1, d), lambda i: (0, 0)),
                  pl.BlockSpec((d, LANES), lambda i: (0, 0))],
        out_specs=[pl.BlockSpec((tm, d), lambda i: (i, 0)), pl.BlockSpec((tm, LANES), lambda i: (i, 0))],
        out_shape=[jax.ShapeDtypeStruct((n, d), F32), jax.ShapeDtypeStruct((n, LANES), F32)],
        compiler_params=_cparams(("arbitrary",)),
        name="moe_router",
    )(x, g, wr16)


def _moe(x, g, wr16, wg16, wu16, wd16, gf, tm, bm, tf, tc, final):
    n, d = x.shape
    h32, info = _router(x, g, wr16, tm)
    n_as = 2 * n
    flat_e = info[:, 0:2].astype(I32).reshape(n_as)
    onehot = (flat_e[:, None] == jnp.arange(N_EXPERTS, dtype=I32)[None, :]).astype(I32)
    csum = jnp.cumsum(onehot, axis=0)
    rank = jnp.sum(onehot * (csum - 1), axis=1)
    counts = csum[-1]
    padded = (counts + bm - 1) // bm * bm
    pad_end = jnp.cumsum(padded)
    pad_start = pad_end - padded
    dest = (jnp.sum(onehot * pad_start[None, :], axis=1) + rank).astype(I32)
    nblk = -(-n_as // bm) + N_EXPERTS
    row_tok = jnp.zeros((nblk * bm,), I32).at[dest].set(jnp.arange(n_as, dtype=I32) // 2)
    blk_start = jnp.arange(nblk, dtype=I32) * bm
    blk_e = jnp.minimum(jnp.sum((blk_start[:, None] >= pad_end[None, :]).astype(I32), axis=1),
                        N_EXPERTS - 1).astype(I32)
    n_used = (pad_end[-1:] // bm).astype(I32)
    ys = _ffn_experts(blk_e, n_used, row_tok, h32, wg16, wu16, wd16, bm, tf)
    return _moe_combine(x, info, gf, ys, dest, tc, final)


def _rmsout_kernel(x_ref, g_ref, o_ref):
    o_ref[...] = _rms(x_ref[...], g_ref[...])


def _rms_out(x, g, tm):
    n, d = x.shape
    return pl.pallas_call(
        _rmsout_kernel,
        grid=(n // tm,),
        in_specs=[pl.BlockSpec((tm, d), lambda i: (i, 0)), pl.BlockSpec((1, d), lambda i: (0, 0))],
        out_specs=pl.BlockSpec((tm, d), lambda i: (i, 0)),
        out_shape=jax.ShapeDtypeStruct((n, d), F32),
        compiler_params=_cparams(("arbitrary",)),
        name="final_norm",
    )(x, g)


def _dsa_score_s_kernel(pt_ref, qi_ref, wi_ref, kn_ref, *rest, n_pages, page):
    pages = rest[:n_pages]
    o_ref = rest[n_pages]
    qi = qi_ref[0]
    wi = wi_ref[0]
    slabs = []
    for p in range(n_pages):
        r = _dot(qi, pages[p][0, 0].astype(BF16))
        slabs.append(jnp.sum(wi * jnp.maximum(r, 0.0), axis=0, keepdims=True))
    kn = kn_ref[0].astype(BF16).astype(F32)
    rn = jnp.sum(qi.astype(F32) * kn, axis=1, keepdims=True)
    sn = jnp.sum(wi * jnp.maximum(rn, 0.0), axis=0, keepdims=True)
    lane = lax.broadcasted_iota(I32, (1, page), 1)
    slabs.append(jnp.where(lane == 0, sn, NEG))
    o_ref[0] = jnp.concatenate(slabs, axis=1)


def _dsa_score_sample(page_table, qi3, wi3, kinew, cache_kidx, layer):
    nb, n_pages = page_table.shape
    page = cache_kidx.shape[3]
    width = (n_pages + 1) * page
    pspecs = [pl.BlockSpec((1, 1, IDX_DIM, page), functools.partial(
        lambda b, pt, p: (layer, pt[b, p], 0, 0), p=p)) for p in range(n_pages)]
    gs = pltpu.PrefetchScalarGridSpec(
        num_scalar_prefetch=1,
        grid=(nb,),
        in_specs=[pl.BlockSpec((1, IDX_HEADS, IDX_DIM), lambda b, pt: (b, 0, 0)),
                  pl.BlockSpec((1, IDX_HEADS, 1), lambda b, pt: (b, 0, 0)),
                  pl.BlockSpec((1, 1, IDX_DIM), lambda b, pt: (b, 0, 0))] + pspecs,
        out_specs=pl.BlockSpec((1, 1, width), lambda b, pt: (b, 0, 0)),
    )
    return pl.pallas_call(
        functools.partial(_dsa_score_s_kernel, n_pages=n_pages, page=page),
        grid_spec=gs,
        out_shape=jax.ShapeDtypeStruct((nb, 1, width), F32),
        compiler_params=_cparams(("arbitrary",)),
        name="dsa_score_sample",
    )(page_table, qi3, wi3, kinew, *([cache_kidx] * n_pages))


def _dsa_select_s_kernel(s_ref, o_ref, sc_ref, *, n_valid, n_sel):
    nch, nkeys, nq = sc_ref.shape
    kpos = lax.broadcasted_iota(I32, (nkeys, nq), 0)
    for c in range(nch):
        sc_ref[c] = jnp.where(c * nkeys + kpos < n_valid, s_ref[c], -jnp.inf)
    tau = _select_threshold(sc_ref, nch, nkeys, nq, n_sel)
    for c in range(nch):
        o_ref[c] = jnp.where(sc_ref[c] >= tau, 0.0, NEG)


def _dsa_select_sample(scores_t, n_valid, n_sel):
    return pl.pallas_call(
        functools.partial(_dsa_select_s_kernel, n_valid=n_valid, n_sel=n_sel),
        out_shape=jax.ShapeDtypeStruct(scores_t.shape, F32),
        scratch_shapes=[pltpu.VMEM(scores_t.shape, F32)],
        name="dsa_select_sample",
    )(scores_t)


def _decode_attend(q, knew, vnew, k_pages, v_pages, bias_fn, bias_new, nrows):
    page = k_pages[0].shape[3]
    qrows, keep = _rows_by_head(q.astype(F32), nrows, HEAD_DIM)
    q16 = qrows.astype(BF16)
    s = jnp.concatenate(
        [_dot(q16, kp[0, 0].astype(BF16)) + bias_fn(p) for p, kp in enumerate(k_pages)], axis=1)
    kn = knew.astype(BF16).astype(F32)
    sn = jnp.sum(qrows * kn, axis=1, keepdims=True) + bias_new
    m = jnp.maximum(jnp.max(s, axis=1, keepdims=True), sn)
    pr = jnp.exp(s - m)
    pn = jnp.exp(sn - m)
    l = jnp.sum(pr, axis=1, keepdims=True) + pn
    acc = pn * vnew.astype(BF16).astype(F32)
    for p, vp in enumerate(v_pages):
        acc = acc + _dot_nt(pr[:, p * page:(p + 1) * page].astype(BF16), vp[0, 0].astype(BF16))
    return jnp.sum(jnp.where(keep, acc / l, 0.0), axis=0, keepdims=True)


def _decode_kernel(pt_ref, qa_ref, qb_ref, kan_ref, van_ref, kbn_ref, vbn_ref, fb_ref, dm_ref, *rest,
                   n_pages, page):
    fk = rest[0:n_pages]
    fv = rest[n_pages:2 * n_pages]
    dk = rest[2 * n_pages:3 * n_pages]
    dv = rest[3 * n_pages:4 * n_pages]
    oa_ref, ob_ref = rest[4 * n_pages:]
    oa = _decode_attend(qa_ref[0], kan_ref[0], van_ref[0], fk, fv,
                        lambda p: fb_ref[0, p], 0.0, FOX_HEADS)
    oa_ref[0] = oa.astype(BF16)
    ob = _decode_attend(qb_ref[0], kbn_ref[0], vbn_ref[0], dk, dv,
                        lambda p: dm_ref[0, :, p * page:(p + 1) * page],
                        dm_ref[0, :, n_pages * page:n_pages * page + 1], 8)
    ob_ref[0] = ob.astype(BF16)


def _decode_attention(page_table, qa3, qb3, kan, van, kbn, vbn, fbias, dmask,
                      cache_fk, cache_fv, cache_dk, cache_dv, layer):
    nb, n_pages = page_table.shape
    page = cache_fk.shape[3]
    wa = FOX_HEADS * HEAD_DIM
    wb = DSA_HEADS * HEAD_DIM

    def pspecs(width):
        return [pl.BlockSpec((1, 1, width, page), functools.partial(
            lambda b, pt, p: (layer, pt[b, p], 0, 0), p=p)) for p in range(n_pages)]

    vec = lambda w: pl.BlockSpec((1, 1, w), lambda b, pt: (b, 0, 0))
    gs = pltpu.PrefetchScalarGridSpec(
        num_scalar_prefetch=1,
        grid=(nb,),
        in_specs=[vec(wa), vec(wb), vec(wa), vec(wa), vec(wb), vec(wb),
                  pl.BlockSpec((1, n_pages, FOX_HEADS, page), lambda b, pt: (b, 0, 0, 0)),
                  vec(dmask.shape[2])] + pspecs(wa) + pspecs(wa) + pspecs(wb) + pspecs(wb),
        out_specs=[vec(wa), vec(wb)],
    )
    return pl.pallas_call(
        functools.partial(_decode_kernel, n_pages=n_pages, page=page),
        grid_spec=gs,
        out_shape=[jax.ShapeDtypeStruct((nb, 1, wa), BF16), jax.ShapeDtypeStruct((nb, 1, wb), BF16)],
        compiler_params=_cparams(("arbitrary",)),
        name="decode_attention",
    )(page_table, qa3, qb3, kan, van, kbn, vbn, fbias, dmask,
      *([cache_fk] * n_pages + [cache_fv] * n_pages + [cache_dk] * n_pages + [cache_dv] * n_pages))


def _prep_w_in(w_in):
    fw, dw = FOX_HEADS * HEAD_DIM, DSA_HEADS * HEAD_DIM
    o_fa = 3 * fw
    o_qb = o_fa + FOX_HEADS
    o_qi = o_qb + 3 * dw
    o_ki = o_qi + IDX_HEADS * IDX_DIM
    o_wi = o_ki + IDX_DIM
    o_u = o_wi + IDX_HEADS
    d = w_in.shape[0]
    misc = jnp.concatenate([w_in[:, o_ki:o_wi], w_in[:, o_fa:o_qb], w_in[:, o_wi:o_u],
                            jnp.zeros((d, LANES - IDX_DIM - FOX_HEADS - IDX_HEADS), w_in.dtype)], axis=1)
    return jnp.concatenate([w_in[:, 0:o_fa], w_in[:, o_qb:o_qi], w_in[:, o_qi:o_ki], misc,
                            w_in[:, o_u:]], axis=1).astype(BF16)


def _prep_w_in_t(w_in):
    fw, dw = FOX_HEADS * HEAD_DIM, DSA_HEADS * HEAD_DIM
    o_fa = 3 * fw
    o_qb = o_fa + FOX_HEADS
    o_kb = o_qb + dw
    o_qi = o_qb + 3 * dw
    o_ki = o_qi + IDX_HEADS * IDX_DIM
    o_wi = o_ki + IDX_DIM
    o_u = o_wi + IDX_HEADS
    d = w_in.shape[0]
    misc = jnp.concatenate([w_in[:, o_ki:o_wi], w_in[:, o_fa:o_qb], w_in[:, o_wi:o_u],
                            jnp.zeros((d, LANES - IDX_DIM - FOX_HEADS - IDX_HEADS), w_in.dtype)], axis=1)
    wq = jnp.concatenate([w_in[:, 0:fw], w_in[:, o_qb:o_kb], w_in[:, o_qi:o_ki], w_in[:, o_u:],
                          w_in[:, o_kb:o_kb + dw]] + [w_in[:, o_ki:o_wi]] * IDX_HEADS, axis=1).astype(BF16)
    wt = jnp.concatenate([w_in[:, fw:o_fa], w_in[:, o_kb:o_qi], misc], axis=1).T.astype(BF16)
    return wq, wt


def _block_diag(w_pool):
    g, c, _ = w_pool.shape
    out = jnp.zeros((g * c, g * c), w_pool.dtype)
    for i in range(g):
        out = out.at[i * c:(i + 1) * c, i * c:(i + 1) * c].set(w_pool[i])
    return out.astype(BF16)


def kernel(x_prompt, x_sample, cache_fox_k, cache_fox_v, cache_fox_logf, cache_dsa_k, cache_dsa_v,
           cache_dsa_kidx, state_pool, cache_mem_k, cache_mem_v, page_table, mem_prompt,
           norm_mix, w_in, b_forget, w_pool, pool_scale, w_out, norm_x, norm_mem, w_xq, w_xkv, w_xo,
           norm_ffn, ffn_w_gate, ffn_w_up, ffn_w_down, moe_router, moe_w_gate, moe_w_up, moe_w_down,
           norm_final):
    b, t, d = x_prompt.shape
    nb = x_sample.shape[0]
    depth = w_in.shape[0]
    n_pool, page = cache_fox_k.shape[1], cache_fox_k.shape[2]
    n_pages = page_table.shape[1]
    n_past = n_pages * page
    n_mem = mem_prompt.shape[1]
    fw, dw = FOX_HEADS * HEAD_DIM, DSA_HEADS * HEAD_DIM
    xw = X_HEADS * HEAD_DIM
    n = b * t

    ch, tq_fox, tq_dsa, tt_pool, tm_mix = 512, 512, 256, 512, 512
    tm_ffn, tf_dense, tm_moe, bm_moe, tf_moe, tc_moe = 1024, 256, 512, 512, 512, 256
    nt = t // ch
    gf = norm_final.reshape(1, d)

    pos_p = jnp.arange(t, dtype=I32)
    pos_s = jnp.full((nb,), n_past, I32)
    tab_p = (_rope_table(pos_p, HEAD_DIM, HEAD_DIM // 4, LANES), _rope_table(pos_p, IDX_DIM, IDX_DIM // 4, LANES),
             _rope_table_t(pos_p, HEAD_DIM, HEAD_DIM // 4), _rope_table_t(pos_p, IDX_DIM, IDX_DIM // 4))
    tab_s = (_rope_table(pos_s, HEAD_DIM, HEAD_DIM // 4, LANES), _rope_table(pos_s, IDX_DIM, IDX_DIM // 4, LANES),
             _rope_table(pos_s, IDX_DIM, IDX_DIM // 4, IDX_DIM))

    cfk = jnp.transpose(cache_fox_k, (0, 1, 3, 4, 2)).reshape(depth, n_pool, fw, page)
    cfv = jnp.transpose(cache_fox_v, (0, 1, 3, 4, 2)).reshape(depth, n_pool, fw, page)
    cdk = jnp.transpose(cache_dsa_k, (0, 1, 3, 4, 2)).reshape(depth, n_pool, dw, page)
    cdv = jnp.transpose(cache_dsa_v, (0, 1, 3, 4, 2)).reshape(depth, n_pool, dw, page)
    ckidx = jnp.transpose(cache_dsa_kidx, (0, 1, 3, 2))
    clogf = jnp.transpose(cache_fox_logf, (0, 1, 3, 2))
    cmk = jnp.transpose(cache_mem_k, (0, 1, 3, 4, 2)).reshape(depth, nb, xw, n_mem)
    cmv = jnp.transpose(cache_mem_v, (0, 1, 3, 4, 2)).reshape(depth, nb, xw, n_mem)
    pool_t = jnp.transpose(state_pool, (0, 2, 1, 3))

    xp = x_prompt.reshape(n, d)
    xs = x_sample.reshape(nb, d)
    st_p, st_s = [], []
    for l in range(depth):
        wp = _prep_w_in(w_in[l])
        wq, wt = _prep_w_in_t(w_in[l])
        bfp = jnp.zeros((1, LANES), F32).at[0, _MISC_LF:_MISC_WI].set(b_forget[l].astype(F32))
        g_mix = norm_mix[l].reshape(1, d)
        g_x = norm_x[l].reshape(1, d)
        g_ffn = norm_ffn[l].reshape(1, d)
        wblk = _block_diag(w_pool[l])
        pscale = pool_scale[l].reshape(1, -1).astype(F32)
        wo16 = w_out[l].astype(BF16)
        wxq16 = w_xq[l].astype(BF16)
        wxo16 = w_xo[l].astype(BF16)

        (qa16, qb16, qi16, u, kb16, kir16, kat, vat, kbt, vbt, misct, ka16t, va16t, vb16t) = _proj_t(
            xp, g_mix, wq, wt, *tab_p, bfp.reshape(LANES, 1), b, t, ch)
        lft = misct[:, _MISC_LF:_MISC_WI, :]
        cft4 = jnp.cumsum(lft, axis=2).reshape(b, FOX_HEADS, nt, ch)
        oa = _fox_prompt(qa16, ka16t, va16t, cft4, b, t, tq_fox, ch)
        ob = _dsa_prompt(qi16, misct, kir16, qb16, kb16, vb16t, b, t, tq_dsa, ch)
        oc = _pool_prompt(u, wblk, pscale, b, t, tt_pool)
        kvt, kvt16 = _norm_matmul_t(mem_prompt.reshape(b * n_mem, d), norm_mem[l].reshape(1, d),
                                    w_xkv[l].T.astype(BF16), b, n_mem)
        xp = _mixx_prompt(xp, oa, ob, oc, wo16, g_x, wxq16, kvt16[:, :xw], kvt16[:, xw:], wxo16, t, tm_mix)
        st_p.append((kat, vat, lft, kbt, vbt, misct[:, _MISC_KI:_MISC_KI + IDX_DIM, :],
                     u.reshape(b, t, -1)[:, -POOL_BUF:], kvt[:, :xw], kvt[:, xw:]))

        (ka_s, va_s, kb_s, vb_s, misc_s, u_s, qa16s, _, _, qb16s, _, _, qi16s) = _proj(
            xs, g_mix, wp, *tab_s, bfp, nb, 1)
        lf_s = misc_s[:, _MISC_LF:_MISC_WI]
        ki_s = misc_s[:, _MISC_KI:_MISC_KI + IDX_DIM]
        wi_s = misc_s[:, _MISC_WI:_MISC_WI + IDX_HEADS]
        lf_past = jnp.transpose(clogf[l][page_table], (0, 2, 1, 3)).reshape(nb, FOX_HEADS, n_past).astype(F32)
        csum = jnp.cumsum(lf_past, axis=2)
        fbias = lf_s[:, :, None] + (csum[:, :, -1:] - csum)
        fbias = jnp.transpose(fbias.reshape(nb, FOX_HEADS, n_pages, page), (0, 2, 1, 3))
        scores = _dsa_score_sample(page_table, qi16s.reshape(nb, IDX_HEADS, IDX_DIM),
                                   wi_s.reshape(nb, IDX_HEADS, 1), ki_s.reshape(nb, 1, IDX_DIM), ckidx, l)
        scores_t = jnp.transpose(scores.reshape(nb, n_pages + 1, page), (1, 2, 0))
        dmask = _dsa_select_sample(scores_t, n_past + 1, min(TOPK_MAX, (n_past + 1) // 4))
        dmask = jnp.transpose(dmask, (2, 0, 1))
        oa_s, ob_s = _decode_attention(
            page_table, qa16s.reshape(nb, 1, fw), qb16s.reshape(nb, 1, dw),
            ka_s.reshape(nb, 1, fw), va_s.reshape(nb, 1, fw), kb_s.reshape(nb, 1, dw),
            vb_s.reshape(nb, 1, dw), fbias, dmask.reshape(nb, 1, -1), cfk, cfv, cdk, cdv, l)
        oc_s = _pool_sample(pool_t[l], u_s, wblk, pscale)
        xs = _mixx_sample(xs, oa_s.reshape(nb, fw), ob_s.reshape(nb, dw), oc_s, wo16, g_x, wxq16,
                          cmk, cmv, l, wxo16, 8)
        st_s.append((ka_s.reshape(nb, 1, FOX_HEADS, HEAD_DIM), va_s.reshape(nb, 1, FOX_HEADS, HEAD_DIM),
                     lf_s.reshape(nb, 1, FOX_HEADS), kb_s.reshape(nb, 1, DSA_HEADS, HEAD_DIM),
                     vb_s.reshape(nb, 1, DSA_HEADS, HEAD_DIM), ki_s.reshape(nb, 1, IDX_DIM),
                     jnp.concatenate([pool_t[l][1:], u_s[None]], axis=0)))

        i = l // 2
        last = l == depth - 1
        if l % 2 == 0:
            wg16 = ffn_w_gate[i:i + 1].astype(BF16)
            wu16 = ffn_w_up[i:i + 1].astype(BF16)
            wd16 = ffn_w_down[i:i + 1].astype(BF16)
            xp = _ffn_dense(xp, g_ffn, wg16, wu16, wd16, tm_ffn, tf_dense)
            xs = _ffn_dense(xs, g_ffn, wg16, wu16, wd16, nb, tf_dense)
            if last:
                xp = _rms_out(xp, gf, tm_ffn)
                xs = _rms_out(xs, gf, nb)
        else:
            wr16 = jnp.zeros((d, LANES), F32).at[:, :N_EXPERTS].set(moe_router[i].astype(F32))
            wg, wu, wd = moe_w_gate[i], moe_w_up[i], moe_w_down[i]
            xp = _moe(xp, g_ffn, wr16, wg, wu, wd, gf, tm_moe, bm_moe, tf_moe, tc_moe, last)
            xs = _moe(xs, g_ffn, wr16, wg, wu, wd, gf, nb, 128, tf_moe, nb, last)

    y_prompt = xp.reshape(b, t, d)
    y_sample = xs.reshape(nb, 1, d)
    stack = lambda states, j: jnp.stack([s[j] for s in states], axis=0)

    def heads_last(a, heads):
        dp, bb, _, pp = a.shape
        return jnp.transpose(a.reshape(dp, bb, heads, HEAD_DIM, pp), (0, 1, 4, 2, 3))

    swap = lambda a: jnp.transpose(a, (0, 1, 3, 2))
    outs_p = (heads_last(stack(st_p, 0), FOX_HEADS), heads_last(stack(st_p, 1), FOX_HEADS),
              swap(stack(st_p, 2)), heads_last(stack(st_p, 3), DSA_HEADS),
              heads_last(stack(st_p, 4), DSA_HEADS), swap(stack(st_p, 5)), stack(st_p, 6),
              heads_last(stack(st_p, 7), X_HEADS), heads_last(stack(st_p, 8), X_HEADS))
    outs_s = tuple(stack(st_s, j) for j in range(6)) + (jnp.transpose(stack(st_s, 6), (0, 2, 1, 3)),)
    return (y_prompt, y_sample) + outs_p + outs_s
```

```python
import functools

import jax
import jax.numpy as jnp
import numpy as np
from jax import lax
from jax.experimental import pallas as pl
from jax.experimental.pallas import tpu as pltpu

F32 = jnp.float32
BF16 = jnp.bfloat16
I32 = jnp.int32

LANES = 128
HEAD_DIM = 64
FOX_HEADS = 8
DSA_HEADS = 4
IDX_HEADS = 8
IDX_DIM = 32
X_HEADS = 4
POOL_WINDOWS = (2, 4, 8, 16)
POOL_CH = 64
POOL_BUF = POOL_WINDOWS[-1] - 1
TOPK_MAX = 256
ROPE_THETA = 500000.0
N_EXPERTS = 8
EPS = 1e-6
NEG = -1e30
INT_MIN = -(2 ** 31)
VMEM_LIMIT = 56 * 1024 * 1024

_PW = (0, 512, 1024, 1536, 1792, 2048, 2304, 2560, 2688, 2944)
_MISC_KI, _MISC_LF, _MISC_WI = 0, 32, 40


def _cparams(sem):
    return pltpu.CompilerParams(dimension_semantics=sem, vmem_limit_bytes=VMEM_LIMIT)


def _dot(a, b):
    return jnp.dot(a, b, preferred_element_type=F32)


def _dot_nt(a, b):
    return lax.dot_general(a, b, (((1,), (1,)), ((), ())), preferred_element_type=F32)


def _rms(x, g):
    return x * lax.rsqrt(jnp.mean(x * x, axis=-1, keepdims=True) + EPS) * g


def _head_mask(shape, head, width, dtype):
    lane = lax.broadcasted_iota(I32, shape, len(shape) - 1)
    return jnp.where((lane >= head * width) & (lane < (head + 1) * width), 1.0, 0.0).astype(dtype)


def _rope(z, tab, half):
    n = z.shape[1]
    rep = n // LANES

    def wide(a):
        return a if rep == 1 else jnp.concatenate([a] * rep, axis=1)

    c = wide(tab[:, 0:LANES])
    sa = wide(tab[:, LANES:2 * LANES])
    sb = wide(tab[:, 2 * LANES:3 * LANES])
    return z * c + pltpu.roll(z, n - half, 1) * sa + pltpu.roll(z, half, 1) * sb


def _proj_kernel(x_ref, g_ref, w_ref, td_ref, ti_ref, tm_ref, bf_ref,
                 ka_ref, va_ref, kb_ref, vb_ref, misc_ref, u_ref,
                 qa16, ka16, va16, qb16, kb16, vb16, qi16):
    h = _rms(x_ref[...], g_ref[...]).astype(BF16)

    def mm(i):
        return _dot(h, w_ref[:, _PW[i]:_PW[i + 1]])

    scale = HEAD_DIM ** -0.5
    qa16[...] = (mm(0) * scale).astype(BF16)
    z = mm(1)
    ka_ref[...] = z
    ka16[...] = z.astype(BF16)
    z = mm(2)
    va_ref[...] = z
    va16[...] = z.astype(BF16)
    td = td_ref[...]
    qb16[...] = (_rope(mm(3), td, 8) * scale).astype(BF16)
    z = _rope(mm(4), td, 8)
    kb_ref[...] = z
    kb16[...] = z.astype(BF16)
    z = mm(5)
    vb_ref[...] = z
    vb16[...] = z.astype(BF16)
    qi16[...] = _rope(mm(6), ti_ref[...], 4).astype(BF16)
    z = mm(7)
    zr = _rope(z, tm_ref[...], 4)
    a = z + bf_ref[...]
    logsig = jnp.minimum(a, 0.0) - jnp.log(1.0 + jnp.exp(-jnp.abs(a)))
    lane = lax.broadcasted_iota(I32, z.shape, 1)
    misc_ref[...] = jnp.where((lane >= _MISC_LF) & (lane < _MISC_WI), logsig, zr)
    u_ref[...] = mm(8)


def _proj(x, g, wp, td, ti, tmi, bfp, tm, n_pos_tiles):
    n, d = x.shape
    grid = (n // tm,)
    row = lambda i: (i, 0)
    const = lambda i: (0, 0)
    tab = lambda i: (i % n_pos_tiles, 0)
    f32_w = (512, 512, 256, 256, 128, 256)
    b16_w = (512, 512, 512, 256, 256, 256, 256)
    out_shape = [jax.ShapeDtypeStruct((n, w), F32) for w in f32_w] + \
                [jax.ShapeDtypeStruct((n, w), BF16) for w in b16_w]
    out_specs = [pl.BlockSpec((tm, w), row) for w in f32_w + b16_w]
    return pl.pallas_call(
        _proj_kernel,
        grid=grid,
        in_specs=[pl.BlockSpec((tm, d), row), pl.BlockSpec((1, d), const),
                  pl.BlockSpec(wp.shape, const),
                  pl.BlockSpec((tm, 3 * LANES), tab), pl.BlockSpec((tm, 3 * LANES), tab),
                  pl.BlockSpec((tm, 3 * LANES), tab), pl.BlockSpec((1, LANES), const)],
        out_specs=out_specs,
        out_shape=out_shape,
        compiler_params=_cparams(("arbitrary",)),
        name="proj_in",
    )(x, g, wp, td, ti, tmi, bfp)


def _rope_table(pos, head, rot, active):
    half = rot // 2
    inv = jnp.float32(ROPE_THETA) ** (-jnp.arange(half, dtype=jnp.float32) / half)
    ang = pos.astype(jnp.float32)[:, None] * inv[None, :]
    cos, sin = jnp.cos(ang), jnp.sin(ang)
    lane = np.arange(LANES)
    jj = lane % head
    idx = jj % half
    on = lane < active
    c = jnp.where((jj < rot) & on, cos[:, idx], 1.0)
    sa = jnp.where((jj < half) & on, -sin[:, idx], 0.0)
    sb = jnp.where((jj >= half) & (jj < rot) & on, sin[:, idx], 0.0)
    return jnp.concatenate([c, sa, sb], axis=1).astype(F32)


def _rope_t(z, tab, half, period):
    f = z.shape[0]
    rep = f // period

    def tall(a):
        return a if rep == 1 else jnp.concatenate([a] * rep, axis=0)

    c = tall(tab[0:period])
    sa = tall(tab[period:2 * period])
    sb = tall(tab[2 * period:3 * period])
    up = jnp.concatenate([z[half:], z[:half]], axis=0)
    dn = jnp.concatenate([z[f - half:], z[:f - half]], axis=0)
    return z * c + up * sa + dn * sb


def _log_sigmoid(a):
    return jnp.minimum(a, 0.0) - jnp.log(1.0 + jnp.exp(-jnp.abs(a)))


_QW = (0, 512, 768, 1024, 1280, 1536, 1792)
_TW = (0, 512, 1024, 1280, 1536, 1664)


def _proj_t_kernel(x_ref, g_ref, wq_ref, wt_ref, td_ref, ti_ref, tdt_ref, tit_ref, bfc_ref,
                   qa16, qb16, qi16, u_ref, kb16, kir16,
                   kat, vat, kbt, vbt, misct, ka16t, va16t, vb16t):
    h = _rms(x_ref[...], g_ref[...]).astype(BF16)

    def mq(i):
        return _dot(h, wq_ref[:, _QW[i]:_QW[i + 1]])

    def mt(i):
        return _dot_nt(wt_ref[_TW[i]:_TW[i + 1], :], h)

    scale = HEAD_DIM ** -0.5
    td = td_ref[...]
    ti = ti_ref[...]
    qa16[...] = (mq(0) * scale).astype(BF16)
    qb16[...] = (_rope(mq(1), td, 8) * scale).astype(BF16)
    qi16[...] = _rope(mq(2), ti, 4).astype(BF16)
    u_ref[...] = mq(3)
    kb16[...] = _rope(mq(4), td, 8).astype(BF16)
    kir16[...] = _rope(mq(5), ti, 4).astype(BF16)
    z = mt(0)
    kat[0] = z
    ka16t[0, 0] = z.astype(BF16)
    z = mt(1)
    vat[0] = z
    va16t[0, 0] = z.astype(BF16)
    kbt[0] = _rope_t(mt(2), tdt_ref[...], 8, HEAD_DIM)
    z = mt(3)
    vbt[0] = z
    vb16t[0, 0] = z.astype(BF16)
    z = mt(4)
    ki = _rope_t(z[0:IDX_DIM], tit_ref[...], 4, IDX_DIM)
    row = lax.broadcasted_iota(I32, z.shape, 0)
    rest = jnp.where((row >= _MISC_LF) & (row < _MISC_WI), _log_sigmoid(z + bfc_ref[...]), z)
    misct[0] = jnp.concatenate([ki, rest[IDX_DIM:]], axis=0)


def _proj_t(x, g, wq, wt, td, ti, tdt, tit, bfc, b, t, tm):
    n, d = x.shape
    nt = t // tm
    row = lambda bi, ti_: (bi * nt + ti_, 0)
    const = lambda bi, ti_: (0, 0)
    tab = lambda bi, ti_: (ti_, 0)
    tabt = lambda bi, ti_: (0, ti_)
    fm = lambda bi, ti_: (bi, 0, ti_)
    ch = lambda bi, ti_: (bi, ti_, 0, 0)
    row_out = [(512, BF16), (256, BF16), (256, BF16), (256, F32), (256, BF16), (256, BF16)]
    fm_out = [512, 512, 256, 256, 128]
    ch_out = [512, 512, 256]
    out_shape = ([jax.ShapeDtypeStruct((n, w), dt) for w, dt in row_out]
                 + [jax.ShapeDtypeStruct((b, w, t), F32) for w in fm_out]
                 + [jax.ShapeDtypeStruct((b, nt, w, tm), BF16) for w in ch_out])
    out_specs = ([pl.BlockSpec((tm, w), row) for w, _ in row_out]
                 + [pl.BlockSpec((1, w, tm), fm) for w in fm_out]
                 + [pl.BlockSpec((1, 1, w, tm), ch) for w in ch_out])
    return pl.pallas_call(
        _proj_t_kernel,
        grid=(b, nt),
        in_specs=[pl.BlockSpec((tm, d), row), pl.BlockSpec((1, d), const),
                  pl.BlockSpec(wq.shape, const), pl.BlockSpec(wt.shape, const),
                  pl.BlockSpec((tm, 3 * LANES), tab), pl.BlockSpec((tm, 3 * LANES), tab),
                  pl.BlockSpec((3 * HEAD_DIM, tm), tabt), pl.BlockSpec((3 * IDX_DIM, tm), tabt),
                  pl.BlockSpec((LANES, 1), const)],
        out_specs=out_specs,
        out_shape=out_shape,
        compiler_params=_cparams(("arbitrary", "arbitrary")),
        name="proj_in_prompt",
    )(x, g, wq, wt, td, ti, tdt, tit, bfc)


def _rope_table_t(pos, head, rot):
    half = rot // 2
    inv = jnp.float32(ROPE_THETA) ** (-jnp.arange(half, dtype=jnp.float32) / half)
    ang = pos.astype(jnp.float32)[:, None] * inv[None, :]
    cos, sin = jnp.cos(ang).T, jnp.sin(ang).T
    jj = np.arange(head)
    idx = jj % half
    c = jnp.where((jj < rot)[:, None], cos[idx], 1.0)
    sa = jnp.where((jj < half)[:, None], -sin[idx], 0.0)
    sb = jnp.where(((jj >= half) & (jj < rot))[:, None], sin[idx], 0.0)
    return jnp.concatenate([c, sa, sb], axis=0).astype(F32)


def _nmm_t_kernel(x_ref, g_ref, wt_ref, o_ref, o16_ref):
    z = _dot_nt(wt_ref[...], _rms(x_ref[...], g_ref[...]).astype(BF16))
    o_ref[0] = z
    o16_ref[0] = z.astype(BF16)


def _norm_matmul_t(x, g, wt16, nblk, tm):
    d = x.shape[1]
    m = wt16.shape[0]
    return pl.pallas_call(
        _nmm_t_kernel,
        grid=(nblk,),
        in_specs=[pl.BlockSpec((tm, d), lambda i: (i, 0)), pl.BlockSpec((1, d), lambda i: (0, 0)),
                  pl.BlockSpec((m, d), lambda i: (0, 0))],
        out_specs=[pl.BlockSpec((1, m, tm), lambda i: (i, 0, 0)), pl.BlockSpec((1, m, tm), lambda i: (i, 0, 0))],
        out_shape=[jax.ShapeDtypeStruct((nblk, m, tm), F32), jax.ShapeDtypeStruct((nblk, m, tm), BF16)],
        compiler_params=_cparams(("arbitrary",)),
        name="norm_matmul_t",
    )(x, g, wt16)


def _fox_kernel(q_ref, kt_ref, vt_ref, cft_ref, o_ref, *, tq, ch):
    hp = pl.program_id(1)
    qt = pl.program_id(2)
    q = q_ref[...]
    qms = [q * _head_mask((1, LANES), i, HEAD_DIM, BF16) for i in range(2)]
    per = ch // tq
    kdiag = qt // per
    row = (qt % per) * tq + lax.broadcasted_iota(I32, (tq, ch), 0)
    causal = lax.broadcasted_iota(I32, (tq, ch), 1) <= row

    def step(kt, carry, masked):
        kk = kt_ref[0, kt]
        vv = vt_ref[0, kt]
        out = []
        for i in range(2):
            m, l, acc = carry[3 * i:3 * i + 3]
            ck = cft_ref[0, pl.ds(2 * hp + i, 1), pl.ds(kt, 1), :].reshape(1, ch)
            s = _dot(qms[i], kk) - ck
            if masked:
                s = jnp.where(causal, s, NEG)
            m_new = jnp.maximum(m, jnp.max(s, axis=1, keepdims=True))
            alpha = jnp.exp(m - m_new)
            p = jnp.exp(s - m_new)
            l = alpha * l + jnp.sum(p, axis=1, keepdims=True)
            acc = alpha * acc + _dot_nt(p.astype(BF16), vv)
            out += [m_new, l, acc]
        return tuple(out)

    init = (jnp.full((tq, 1), NEG, F32), jnp.zeros((tq, 1), F32), jnp.zeros((tq, LANES), F32)) * 2
    carry = lax.fori_loop(0, kdiag, functools.partial(step, masked=False), init)
    carry = step(kdiag, carry, True)
    lane = lax.broadcasted_iota(I32, (tq, LANES), 1)
    o_ref[...] = jnp.where(lane < HEAD_DIM, carry[2] / carry[1], carry[5] / carry[4]).astype(BF16)


def _fox_prompt(qa16, ka16t, va16t, cft4, b, t, tq, ch):
    nq = t // tq
    nc = t // ch
    n = b * t
    kv = pl.BlockSpec((1, nc, LANES, ch), lambda bi, hp, qi: (bi, 0, hp, 0))
    return pl.pallas_call(
        functools.partial(_fox_kernel, tq=tq, ch=ch),
        grid=(b, FOX_HEADS // 2, nq),
        in_specs=[pl.BlockSpec((tq, LANES), lambda bi, hp, qi: (bi * nq + qi, hp)), kv, kv,
                  pl.BlockSpec((1, FOX_HEADS, nc, ch), lambda bi, hp, qi: (bi, 0, 0, 0))],
        out_specs=pl.BlockSpec((tq, LANES), lambda bi, hp, qi: (bi * nq + qi, hp)),
        out_shape=jax.ShapeDtypeStruct((n, FOX_HEADS * HEAD_DIM), BF16),
        compiler_params=_cparams(("arbitrary", "arbitrary", "arbitrary")),
        name="fox_prompt",
    )(qa16, ka16t, va16t, cft4)


def _fold_rows(x, op, rows=8):
    while x.shape[0] > rows:
        half = x.shape[0] // 2
        x = op(x[:half], x[half:])
    return x


def _key_to_f32(key):
    bits = key ^ (lax.shift_right_arithmetic(key, 31) & 0x7FFFFFFF)
    return lax.bitcast_convert_type(bits, F32)


def _select_threshold(sc_ref, nch, nkeys, nq, ksel):
    kf = float(ksel)
    lowest = float(np.finfo(np.float32).min)

    def count_ge(cand):
        def body(c, acc):
            return acc + _fold_rows(jnp.where(sc_ref[c] >= cand, 1.0, 0.0), jnp.add)

        acc = lax.fori_loop(0, nch, body, jnp.zeros((8, nq), F32))
        return jnp.sum(acc, axis=0, keepdims=True)

    key = jnp.where(count_ge(jnp.zeros((1, nq), F32)) >= kf, 0, INT_MIN).astype(I32)

    def bit_body(bi, key):
        cand = key | lax.shift_left(jnp.int32(1), 30 - bi)
        return jnp.where(count_ge(_key_to_f32(cand)) >= kf, cand, key)

    key = lax.fori_loop(0, 31, bit_body, key)
    tau = _key_to_f32(key)
    real = tau >= lowest
    tie = real & (count_ge(tau) > kf)

    @pl.when(jnp.max(jnp.where(tie, 1.0, 0.0)) > 0.0)
    def _():
        need = kf - count_ge(_key_to_f32(key + 1))
        r_ = lax.broadcasted_iota(I32, (LANES, LANES), 0)
        c_ = lax.broadcasted_iota(I32, (LANES, LANES), 1)
        tri = jnp.where(c_ < r_, 1.0, 0.0).astype(BF16)

        def body(c, carry):
            for s in range(nkeys // LANES):
                kk = sc_ref[c, s * LANES:(s + 1) * LANES, :]
                eq = kk == tau
                eqf = jnp.where(eq, 1.0, 0.0)
                pre = _dot(tri, eqf.astype(BF16)) + carry
                drop = eq & (pre >= need) & tie
                sc_ref[c, s * LANES:(s + 1) * LANES, :] = jnp.where(drop, -jnp.inf, kk)
                carry = carry + jnp.sum(eqf, axis=0, keepdims=True)
            return carry

        lax.fori_loop(0, nch, body, jnp.zeros((1, nq), F32))

    return jnp.where(real, tau, lowest)


def _dsa_kernel(qi_ref, mt_ref, kir_ref, q_ref, k_ref, vt_ref, o_ref, sc_ref, *, tq, w, n_sel):
    qt = pl.program_id(1)
    nch = ((qt + 1) * tq + w - 1) // w
    qi = qi_ref[...]
    mt = mt_ref[0]
    kpos = lax.broadcasted_iota(I32, (w, tq), 0)
    qpos = qt * tq + lax.broadcasted_iota(I32, (w, tq), 1)
    qms = [qi * _head_mask((1, IDX_HEADS * IDX_DIM), j, IDX_DIM, BF16) for j in range(IDX_HEADS)]
    wrows = [mt[_MISC_WI + j:_MISC_WI + j + 1, :] for j in range(IDX_HEADS)]

    def score_chunk(c, carry):
        kk = kir_ref[pl.ds(pl.multiple_of(c * w, w), w), :]
        sc = jnp.zeros((w, tq), F32)
        for j in range(IDX_HEADS):
            sc = sc + wrows[j] * jnp.maximum(_dot_nt(kk, qms[j]), 0.0)
        sc_ref[c] = jnp.where(c * w + kpos <= qpos, sc, -jnp.inf)
        return carry

    lax.fori_loop(0, nch, score_chunk, 0)
    tau = _select_threshold(sc_ref, nch, w, tq, n_sel)

    q = q_ref[...]
    nl = DSA_HEADS * HEAD_DIM
    qhs = [q * _head_mask((1, nl), h, HEAD_DIM, BF16) for h in range(DSA_HEADS)]

    def step(c, carry):
        kk = k_ref[pl.ds(pl.multiple_of(c * w, w), w), :]
        vv = vt_ref[0, c]
        sel = sc_ref[c] >= tau
        out = []
        for h in range(DSA_HEADS):
            m, l, acc = carry[3 * h:3 * h + 3]
            s = jnp.where(sel, _dot_nt(kk, qhs[h]), NEG)
            m_new = jnp.maximum(m, jnp.max(_fold_rows(s, jnp.maximum), axis=0, keepdims=True))
            alpha = jnp.exp(m - m_new)
            p = jnp.exp(s - m_new)
            l = alpha * l + jnp.sum(_fold_rows(p, jnp.add), axis=0, keepdims=True)
            acc = alpha * acc + _dot(vv, p.astype(BF16))
            out += [m_new, l, acc]
        return tuple(out)

    init = (jnp.full((1, tq), NEG, F32), jnp.zeros((1, tq), F32), jnp.zeros((nl, tq), F32)) * DSA_HEADS
    carry = lax.fori_loop(0, nch, step, init)
    frow = lax.broadcasted_iota(I32, (nl, tq), 0)
    out_t = jnp.zeros((nl, tq), F32)
    for h in range(DSA_HEADS):
        keep = (frow >= h * HEAD_DIM) & (frow < (h + 1) * HEAD_DIM)
        out_t = jnp.where(keep, carry[3 * h + 2] / carry[3 * h + 1], out_t)
    o_ref[...] = out_t.T.astype(BF16)


def _dsa_prompt(qi16, misct, kir16, qb16, kb16, vb16t, b, t, tq, w):
    nq = t // tq
    n = b * t
    nl = DSA_HEADS * HEAD_DIM
    n_sel = min(TOPK_MAX, t // 4)
    qrow = lambda bi, qi: (bi * nq + qi, 0)
    keys = pl.BlockSpec((t, nl), lambda bi, qi: (bi, 0))
    return pl.pallas_call(
        functools.partial(_dsa_kernel, tq=tq, w=w, n_sel=n_sel),
        grid=(b, nq),
        in_specs=[pl.BlockSpec((tq, nl), qrow), pl.BlockSpec((1, LANES, tq), lambda bi, qi: (bi, 0, qi)),
                  keys, pl.BlockSpec((tq, nl), qrow), keys,
                  pl.BlockSpec((1, t // w, nl, w), lambda bi, qi: (bi, 0, 0, 0))],
        out_specs=pl.BlockSpec((tq, nl), qrow),
        out_shape=jax.ShapeDtypeStruct((n, nl), BF16),
        scratch_shapes=[pltpu.VMEM((t // w, w, tq), F32)],
        compiler_params=_cparams(("arbitrary", "arbitrary")),
        name="dsa_prompt",
    )(qi16, misct, kir16, qb16, kb16, vb16t)


def _pool_mix(win_sums, x, cnts, wblk, scale):
    lane = lax.broadcasted_iota(I32, x.shape, 1)
    mean = jnp.zeros_like(x)
    for g, wdw in enumerate(POOL_WINDOWS):
        mg = win_sums[wdw] / cnts[g]
        mean = jnp.where((lane >= g * POOL_CH) & (lane < (g + 1) * POOL_CH), mg, mean)
    return (_dot((mean - x).astype(BF16), wblk) * scale).astype(BF16)


def _pool_kernel(u_ref, wblk_ref, scale_ref, o_ref, ext_ref, *, tt):
    t = pl.program_id(1)
    hist = POOL_WINDOWS[-1]

    @pl.when(t == 0)
    def _():
        ext_ref[0:hist, :] = jnp.zeros((hist, ext_ref.shape[1]), F32)

    x = u_ref[...]
    ext_ref[hist:hist + tt, :] = x
    acc = x
    sums = {}
    for s in range(1, hist):
        acc = acc + ext_ref[hist - s:hist - s + tt, :]
        if s + 1 in POOL_WINDOWS:
            sums[s + 1] = acc
    pos = t * tt + lax.broadcasted_iota(I32, (tt, 1), 0)
    cnts = [jnp.minimum(pos + 1, wdw).astype(F32) for wdw in POOL_WINDOWS]
    o_ref[...] = _pool_mix(sums, x, cnts, wblk_ref[...], scale_ref[...])
    ext_ref[0:hist, :] = ext_ref[tt:tt + hist, :]


def _pool_prompt(u, wblk16, scale, b, t, tt):
    n, c = u.shape
    nt = t // tt
    return pl.pallas_call(
        functools.partial(_pool_kernel, tt=tt),
        grid=(b, nt),
        in_specs=[pl.BlockSpec((tt, c), lambda bi, ti: (bi * nt + ti, 0)),
                  pl.BlockSpec((c, c), lambda bi, ti: (0, 0)),
                  pl.BlockSpec((1, c), lambda bi, ti: (0, 0))],
        out_specs=pl.BlockSpec((tt, c), lambda bi, ti: (bi * nt + ti, 0)),
        out_shape=jax.ShapeDtypeStruct((n, c), BF16),
        scratch_shapes=[pltpu.VMEM((tt + POOL_WINDOWS[-1], c), F32)],
        compiler_params=_cparams(("arbitrary", "arbitrary")),
        name="pool_prompt",
    )(u, wblk16, scale)


def _pool_s_kernel(st_ref, u_ref, wblk_ref, scale_ref, o_ref):
    x = u_ref[...]
    acc = x
    sums = {}
    for s in range(1, POOL_WINDOWS[-1]):
        acc = acc + st_ref[POOL_BUF - s]
        if s + 1 in POOL_WINDOWS:
            sums[s + 1] = acc
    cnts = [float(wdw) for wdw in POOL_WINDOWS]
    o_ref[...] = _pool_mix(sums, x, cnts, wblk_ref[...], scale_ref[...])


def _pool_sample(state_t, u, wblk16, scale):
    n, c = u.shape
    return pl.pallas_call(
        _pool_s_kernel,
        out_shape=jax.ShapeDtypeStruct((n, c), BF16),
        name="pool_sample",
    )(state_t, u, wblk16, scale)


def _mix_out(x, oa, ob, oc, wo_ref):
    na, nb = oa.shape[1], ob.shape[1]
    return (x + _dot(oa, wo_ref[0:na, :]) + _dot(ob, wo_ref[na:na + nb, :])
            + _dot(oc, wo_ref[na + nb:, :]))


def _mixx_kernel(x_ref, oa_ref, ob_ref, oc_ref, wo_ref, gx_ref, wxq_ref, mk_ref, mv_ref, wxo_ref,
                 o_ref):
    x1 = _mix_out(x_ref[...], oa_ref[...], ob_ref[...], oc_ref[...], wo_ref)
    h = _rms(x1, gx_ref[...]).astype(BF16)
    q = (_dot(h, wxq_ref[...]) * HEAD_DIM ** -0.5).astype(BF16)
    mk = mk_ref[0]
    mv = mv_ref[0]
    nl = X_HEADS * HEAD_DIM
    o = jnp.zeros(q.shape, F32)
    for hh in range(X_HEADS):
        hm = _head_mask((1, nl), hh, HEAD_DIM, F32)
        s = _dot(q * hm.astype(BF16), mk)
        p = jnp.exp(s - jnp.max(s, axis=1, keepdims=True))
        l = jnp.sum(p, axis=1, keepdims=True)
        o = o + (_dot_nt(p.astype(BF16), mv) / l) * hm
    o_ref[...] = x1 + _dot(o.astype(BF16), wxo_ref[...])


def _mixx_prompt(x, oa, ob, oc, wo16, gx, wxq16, mk16, mv16, wxo16, t, tm):
    n, d = x.shape
    per_b = t // tm
    row = lambda i: (i, 0)
    const = lambda i: (0, 0)
    mem = lambda i: (i // per_b, 0, 0)
    return pl.pallas_call(
        _mixx_kernel,
        grid=(n // tm,),
        in_specs=[pl.BlockSpec((tm, d), row), pl.BlockSpec((tm, oa.shape[1]), row),
                  pl.BlockSpec((tm, ob.shape[1]), row), pl.BlockSpec((tm, oc.shape[1]), row),
                  pl.BlockSpec(wo16.shape, const), pl.BlockSpec((1, d), const),
                  pl.BlockSpec(wxq16.shape, const),
                  pl.BlockSpec((1,) + mk16.shape[1:], mem), pl.BlockSpec((1,) + mv16.shape[1:], mem),
                  pl.BlockSpec(wxo16.shape, const)],
        out_specs=pl.BlockSpec((tm, d), row),
        out_shape=jax.ShapeDtypeStruct((n, d), F32),
        compiler_params=_cparams(("arbitrary",)),
        name="mix_xattn_prompt",
    )(x, oa, ob, oc, wo16, gx, wxq16, mk16, mv16, wxo16)


def _rows_by_head(v, nrows, width):
    lanes = v.shape[1]
    r = lax.broadcasted_iota(I32, (nrows, lanes), 0)
    lane = lax.broadcasted_iota(I32, (nrows, lanes), 1)
    keep = (lane >= r * width) & (lane < (r + 1) * width)
    return jnp.where(keep, jnp.broadcast_to(v, (nrows, lanes)), 0.0), keep


def _mixx_s_kernel(x_ref, oa_ref, ob_ref, oc_ref, wo_ref, gx_ref, wxq_ref, mk_ref, mv_ref, wxo_ref,
                   o_ref, att_ref, *, sb):
    x1 = _mix_out(x_ref[...], oa_ref[...], ob_ref[...], oc_ref[...], wo_ref)
    h = _rms(x1, gx_ref[...]).astype(BF16)
    q = _dot(h, wxq_ref[...]) * HEAD_DIM ** -0.5
    for b in range(sb):
        qrows, keep = _rows_by_head(q[b:b + 1, :], 8, HEAD_DIM)
        s = _dot(qrows.astype(BF16), mk_ref[0, b].astype(BF16))
        p = jnp.exp(s - jnp.max(s, axis=1, keepdims=True))
        l = jnp.sum(p, axis=1, keepdims=True)
        ob_ = _dot_nt(p.astype(BF16), mv_ref[0, b].astype(BF16)) / l
        att_ref[b:b + 1, :] = jnp.sum(jnp.where(keep, ob_, 0.0), axis=0, keepdims=True)
    o_ref[...] = x1 + _dot(att_ref[...].astype(BF16), wxo_ref[...])


def _mixx_sample(x, oa, ob, oc, wo16, gx, wxq16, memk, memv, layer, wxo16, sb):
    n, d = x.shape
    nl = X_HEADS * HEAD_DIM
    row = lambda i: (i, 0)
    const = lambda i: (0, 0)
    mem = lambda i: (layer, i, 0, 0)
    mblk = (1, sb) + memk.shape[2:]
    return pl.pallas_call(
        functools.partial(_mixx_s_kernel, sb=sb),
        grid=(n // sb,),
        in_specs=[pl.BlockSpec((sb, d), row), pl.BlockSpec((sb, oa.shape[1]), row),
                  pl.BlockSpec((sb, ob.shape[1]), row), pl.BlockSpec((sb, oc.shape[1]), row),
                  pl.BlockSpec(wo16.shape, const), pl.BlockSpec((1, d), const),
                  pl.BlockSpec(wxq16.shape, const),
                  pl.BlockSpec(mblk, mem), pl.BlockSpec(mblk, mem),
                  pl.BlockSpec(wxo16.shape, const)],
        out_specs=pl.BlockSpec((sb, d), row),
        out_shape=jax.ShapeDtypeStruct((n, d), F32),
        scratch_shapes=[pltpu.VMEM((sb, nl), F32)],
        compiler_params=_cparams(("arbitrary",)),
        name="mix_xattn_sample",
    )(x, oa, ob, oc, wo16, gx, wxq16, memk, memv, wxo16)


def _swiglu_step(h, wg, wu, wd):
    a = _dot(h, wg)
    b = _dot(h, wu)
    t = (a * (1.0 / (1.0 + jnp.exp(-a)))) * b
    return _dot(t.astype(BF16), wd)


def _ffn_kernel(x_ref, g_ref, wg_ref, wu_ref, wd_ref, o_ref, h_ref, acc_ref):
    j = pl.program_id(1)

    @pl.when(j == 0)
    def _():
        h_ref[...] = _rms(x_ref[...], g_ref[...]).astype(BF16)
        acc_ref[...] = jnp.zeros(acc_ref.shape, F32)

    acc_ref[...] += _swiglu_step(h_ref[...], wg_ref[0], wu_ref[0], wd_ref[0])

    @pl.when(j == pl.num_programs(1) - 1)
    def _():
        o_ref[...] = x_ref[...] + acc_ref[...]


def _ffn_dense(x, g, wg16, wu16, wd16, bm, tf):
    n, d = x.shape
    dff = wg16.shape[2]
    return pl.pallas_call(
        _ffn_kernel,
        grid=(n // bm, dff // tf),
        in_specs=[pl.BlockSpec((bm, d), lambda i, j: (i, 0)),
                  pl.BlockSpec((1, d), lambda i, j: (0, 0)),
                  pl.BlockSpec((1, d, tf), lambda i, j: (0, 0, j)),
                  pl.BlockSpec((1, d, tf), lambda i, j: (0, 0, j)),
                  pl.BlockSpec((1, tf, d), lambda i, j: (0, j, 0))],
        out_specs=pl.BlockSpec((bm, d), lambda i, j: (i, 0)),
        out_shape=jax.ShapeDtypeStruct((n, d), F32),
        scratch_shapes=[pltpu.VMEM((bm, d), BF16), pltpu.VMEM((bm, d), F32)],
        compiler_params=_cparams(("arbitrary", "arbitrary")),
        name="ffn_dense",
    )(x, g, wg16, wu16, wd16)


def _start_row_gather(src_hbm, idx_ref, dst_ref, sem, n_rows):
    def body(r, carry):
        pltpu.make_async_copy(src_hbm.at[pl.ds(idx_ref[0, 0, r], 1)], dst_ref.at[pl.ds(r, 1)], sem).start()
        return carry

    lax.fori_loop(0, n_rows, body, 0, unroll=8)


def _wait_row_gather(src_hbm, dst_ref, sem, n_rows):
    pltpu.make_async_copy(src_hbm.at[pl.ds(0, n_rows)], dst_ref, sem).wait()


def _experts_kernel(eid_ref, nu_ref, idx_ref, idxn_ref, h_hbm, wg_ref, wu_ref, wd_ref, o_ref,
                    xs_ref, sem, h_ref, acc_ref, *, bm, rp, nf):
    i = pl.program_id(0)
    j = pl.program_id(1)
    slot = i % 2
    n_used = nu_ref[0]
    used = i < n_used
    per = rp // nf

    @pl.when((i == 0) & (j == 0))
    def _():
        _start_row_gather(h_hbm, idx_ref, xs_ref.at[0], sem.at[0], rp)

    @pl.when(used & (j == 0))
    def _():
        _wait_row_gather(h_hbm, xs_ref.at[slot], sem.at[slot], rp)
        h_ref[...] = xs_ref[slot, 0:bm].astype(BF16)
        acc_ref[...] = jnp.zeros(acc_ref.shape, F32)

    @pl.when(used)
    def _():
        base = j * per
        for r in range(per):
            pltpu.make_async_copy(h_hbm.at[pl.ds(idxn_ref[0, 0, base + r], 1)],
                                  xs_ref.at[1 - slot, pl.ds(base + r, 1)], sem.at[1 - slot]).start()
        acc_ref[...] += _swiglu_step(h_ref[...], wg_ref[0], wu_ref[0], wd_ref[0])

    @pl.when(j == nf - 1)
    def _():
        o_ref[...] = jnp.where(used, acc_ref[...], 0.0)

    @pl.when((i == pl.num_programs(0) - 1) & (j == nf - 1))
    def _():
        _wait_row_gather(h_hbm, xs_ref.at[n_used % 2], sem.at[n_used % 2], rp)


def _ffn_experts(blk_e, n_used, row_tok, h32, wg, wu, wd, bm, tf):
    d = h32.shape[1]
    nblk = blk_e.shape[0]
    nf = wg.shape[2] // tf
    per = -(-bm // nf)
    per = -(-per // 8) * 8
    rp = per * nf
    idx3 = jnp.pad(row_tok.reshape(nblk, bm), ((0, 0), (0, rp - bm))).reshape(nblk, 1, rp)
    fj = lambda i, j, nu: jnp.where(i < nu[0], j, nf - 1)
    gs = pltpu.PrefetchScalarGridSpec(
        num_scalar_prefetch=2,
        grid=(nblk, nf),
        in_specs=[pl.BlockSpec((1, 1, rp), lambda i, j, e, nu: (i, 0, 0), memory_space=pltpu.SMEM),
                  pl.BlockSpec((1, 1, rp), lambda i, j, e, nu: (jnp.minimum(i + 1, nu[0] - 1), 0, 0),
                               memory_space=pltpu.SMEM),
                  pl.BlockSpec(memory_space=pl.ANY),
                  pl.BlockSpec((1, d, tf), lambda i, j, e, nu: (e[i], 0, fj(i, j, nu))),
                  pl.BlockSpec((1, d, tf), lambda i, j, e, nu: (e[i], 0, fj(i, j, nu))),
                  pl.BlockSpec((1, tf, d), lambda i, j, e, nu: (e[i], fj(i, j, nu), 0))],
        out_specs=pl.BlockSpec((bm, d), lambda i, j, e, nu: (i, 0)),
        scratch_shapes=[pltpu.VMEM((2, rp, d), F32), pltpu.SemaphoreType.DMA((2,)),
                        pltpu.VMEM((bm, d), BF16), pltpu.VMEM((bm, d), F32)],
    )
    return pl.pallas_call(
        functools.partial(_experts_kernel, bm=bm, rp=rp, nf=nf),
        grid_spec=gs,
        out_shape=jax.ShapeDtypeStruct((nblk * bm, d), F32),
        compiler_params=_cparams(("arbitrary", "arbitrary")),
        name="ffn_experts",
    )(blk_e, n_used, idx3, idx3, h32, wg, wu, wd)


def _combine_kernel(idx_ref, idxn_ref, x_ref, info_ref, g_ref, ys_hbm, o_ref, buf_ref, sem, *, tm, final):
    i = pl.program_id(0)
    slot = i % 2

    @pl.when(i == 0)
    def _():
        _start_row_gather(ys_hbm, idx_ref, buf_ref.at[0], sem.at[0], 2 * tm)

    _wait_row_gather(ys_hbm, buf_ref.at[slot], sem.at[slot], 2 * tm)

    @pl.when(i + 1 < pl.num_programs(0))
    def _():
        _start_row_gather(ys_hbm, idxn_ref, buf_ref.at[1 - slot], sem.at[1 - slot], 2 * tm)

    info = info_ref[...]
    y = x_ref[...] + info[:, 2:3] * buf_ref[slot, 0:tm] + info[:, 3:4] * buf_ref[slot, tm:2 * tm]
    o_ref[...] = _rms(y, g_ref[...]) if final else y


def _moe_combine(x, info, gf, ys, dest2, tm, final):
    n, d = x.shape
    nt = n // tm
    idx3 = jnp.transpose(dest2.reshape(nt, tm, 2), (0, 2, 1)).reshape(nt, 1, 2 * tm)
    return pl.pallas_call(
        functools.partial(_combine_kernel, tm=tm, final=final),
        grid=(nt,),
        in_specs=[pl.BlockSpec((1, 1, 2 * tm), lambda i: (i, 0, 0), memory_space=pltpu.SMEM),
                  pl.BlockSpec((1, 1, 2 * tm), lambda i: (jnp.minimum(i + 1, nt - 1), 0, 0),
                               memory_space=pltpu.SMEM),
                  pl.BlockSpec((tm, d), lambda i: (i, 0)), pl.BlockSpec((tm, LANES), lambda i: (i, 0)),
                  pl.BlockSpec((1, d), lambda i: (0, 0)), pl.BlockSpec(memory_space=pl.ANY)],
        out_specs=pl.BlockSpec((tm, d), lambda i: (i, 0)),
        out_shape=jax.ShapeDtypeStruct((n, d), F32),
        scratch_shapes=[pltpu.VMEM((2, 2 * tm, d), F32), pltpu.SemaphoreType.DMA((2,))],
        compiler_params=_cparams(("arbitrary",)),
        name="moe_combine",
    )(idx3, idx3, x, info, gf, ys)


def _router_kernel(x_ref, g_ref, wr_ref, h_ref, info_ref):
    h = _rms(x_ref[...], g_ref[...])
    h_ref[...] = h
    w = wr_ref[...]
    h_hi = h.astype(BF16)
    h_lo = (h - h_hi.astype(F32)).astype(BF16)
    w_hi = w.astype(BF16)
    w_lo = (w - w_hi.astype(F32)).astype(BF16)
    logits = _dot(h_hi, w_hi) + (_dot(h_hi, w_lo) + _dot(h_lo, w_hi))
    lane = lax.broadcasted_iota(I32, logits.shape, 1)
    logits = jnp.where(lane < N_EXPERTS, logits, NEG)
    m1 = jnp.max(logits, axis=1, keepdims=True)
    i1 = jnp.min(jnp.where(logits == m1, lane, LANES), axis=1, keepdims=True)
    rest = jnp.where(lane == i1, NEG, logits)
    m2 = jnp.max(rest, axis=1, keepdims=True)
    i2 = jnp.min(jnp.where(rest == m2, lane, LANES), axis=1, keepdims=True)
    e = jnp.exp(m2 - m1)
    g1 = 1.0 / (1.0 + e)
    g2 = e / (1.0 + e)
    info = jnp.where(lane == 0, i1.astype(F32), jnp.where(lane == 1, i2.astype(F32),
                     jnp.where(lane == 2, g1, jnp.where(lane == 3, g2, 0.0))))
    info_ref[...] = info


def _router(x, g, wr16, tm):
    n, d = x.shape
    return pl.pallas_call(
        _router_kernel,
        grid=(n // tm,),
        in_specs=[pl.BlockSpec((tm, d), lambda i: (i, 0)), pl.BlockSpec((1, d), lambda i: (0, 0)),
                  pl.BlockSpec((d, LANES), lambda i: (0, 0))],
        out_specs=[pl.BlockSpec((tm, d), lambda i: (i, 0)), pl.BlockSpec((tm, LANES), lambda i: (i, 0))],
        out_shape=[jax.ShapeDtypeStruct((n, d), F32), jax.ShapeDtypeStruct((n, LANES), F32)],
        compiler_params=_cparams(("arbitrary",)),
        name="moe_router",
    )(x, g, wr16)


def _moe(x, g, wr16, wg16, wu16, wd16, gf, tm, bm, tf, tc, final):
    n, d = x.shape
    h32, info = _router(x, g, wr16, tm)
    n_as = 2 * n
    flat_e = info[:, 0:2].astype(I32).reshape(n_as)
    onehot = (flat_e[:, None] == jnp.arange(N_EXPERTS, dtype=I32)[None, :]).astype(I32)
    csum = jnp.cumsum(onehot, axis=0)
    rank = jnp.sum(onehot * (csum - 1), axis=1)
    counts = csum[-1]
    padded = (counts + bm - 1) // bm * bm
    pad_end = jnp.cumsum(padded)
    pad_start = pad_end - padded
    dest = (jnp.sum(onehot * pad_start[None, :], axis=1) + rank).astype(I32)
    nblk = -(-n_as // bm) + N_EXPERTS
    row_tok = jnp.zeros((nblk * bm,), I32).at[dest].set(jnp.arange(n_as, dtype=I32) // 2)
    blk_start = jnp.arange(nblk, dtype=I32) * bm
    blk_e = jnp.minimum(jnp.sum((blk_start[:, None] >= pad_end[None, :]).astype(I32), axis=1),
                        N_EXPERTS - 1).astype(I32)
    n_used = (pad_end[-1:] // bm).astype(I32)
    ys = _ffn_experts(blk_e, n_used, row_tok, h32, wg16, wu16, wd16, bm, tf)
    return _moe_combine(x, info, gf, ys, dest, tc, final)


def _rmsout_kernel(x_ref, g_ref, o_ref):
    o_ref[...] = _rms(x_ref[...], g_ref[...])


def _rms_out(x, g, tm):
    n, d = x.shape
    return pl.pallas_call(
        _rmsout_kernel,
        grid=(n // tm,),
        in_specs=[pl.BlockSpec((tm, d), lambda i: (i, 0)), pl.BlockSpec((1, d), lambda i: (0, 0))],
        out_specs=pl.BlockSpec((tm, d), lambda i: (i, 0)),
        out_shape=jax.ShapeDtypeStruct((n, d), F32),
        compiler_params=_cparams(("arbitrary",)),
        name="final_norm",
    )(x, g)


def _dsa_score_s_kernel(pt_ref, qi_ref, wi_ref, kn_ref, *rest, n_pages, page):
    pages = rest[:n_pages]
    o_ref = rest[n_pages]
    qi = qi_ref[0]
    wi = wi_ref[0]
    slabs = []
    for p in range(n_pages):
        r = _dot(qi, pages[p][0, 0].astype(BF16))
        slabs.append(jnp.sum(wi * jnp.maximum(r, 0.0), axis=0, keepdims=True))
    kn = kn_ref[0].astype(BF16).astype(F32)
    rn = jnp.sum(qi.astype(F32) * kn, axis=1, keepdims=True)
    sn = jnp.sum(wi * jnp.maximum(rn, 0.0), axis=0, keepdims=True)
    lane = lax.broadcasted_iota(I32, (1, page), 1)
    slabs.append(jnp.where(lane == 0, sn, NEG))
    o_ref[0] = jnp.concatenate(slabs, axis=1)


def _dsa_score_sample(page_table, qi3, wi3, kinew, cache_kidx, layer):
    nb, n_pages = page_table.shape
    page = cache_kidx.shape[3]
    width = (n_pages + 1) * page
    pspecs = [pl.BlockSpec((1, 1, IDX_DIM, page), functools.partial(
        lambda b, pt, p: (layer, pt[b, p], 0, 0), p=p)) for p in range(n_pages)]
    gs = pltpu.PrefetchScalarGridSpec(
        num_scalar_prefetch=1,
        grid=(nb,),
        in_specs=[pl.BlockSpec((1, IDX_HEADS, IDX_DIM), lambda b, pt: (b, 0, 0)),
                  pl.BlockSpec((1, IDX_HEADS, 1), lambda b, pt: (b, 0, 0)),
                  pl.BlockSpec((1, 1, IDX_DIM), lambda b, pt: (b, 0, 0))] + pspecs,
        out_specs=pl.BlockSpec((1, 1, width), lambda b, pt: (b, 0, 0)),
    )
    return pl.pallas_call(
        functools.partial(_dsa_score_s_kernel, n_pages=n_pages, page=page),
        grid_spec=gs,
        out_shape=jax.ShapeDtypeStruct((nb, 1, width), F32),
        compiler_params=_cparams(("arbitrary",)),
        name="dsa_score_sample",
    )(page_table, qi3, wi3, kinew, *([cache_kidx] * n_pages))


def _dsa_select_s_kernel(s_ref, o_ref, sc_ref, *, n_valid, n_sel):
    nch, nkeys, nq = sc_ref.shape
    kpos = lax.broadcasted_iota(I32, (nkeys, nq), 0)
    for c in range(nch):
        sc_ref[c] = jnp.where(c * nkeys + kpos < n_valid, s_ref[c], -jnp.inf)
    tau = _select_threshold(sc_ref, nch, nkeys, nq, n_sel)
    for c in range(nch):
        o_ref[c] = jnp.where(sc_ref[c] >= tau, 0.0, NEG)


def _dsa_select_sample(scores_t, n_valid, n_sel):
    return pl.pallas_call(
        functools.partial(_dsa_select_s_kernel, n_valid=n_valid, n_sel=n_sel),
        out_shape=jax.ShapeDtypeStruct(scores_t.shape, F32),
        scratch_shapes=[pltpu.VMEM(scores_t.shape, F32)],
        name="dsa_select_sample",
    )(scores_t)


def _decode_attend(q, knew, vnew, k_pages, v_pages, bias_fn, bias_new, nrows):
    page = k_pages[0].shape[3]
    qrows, keep = _rows_by_head(q.astype(F32), nrows, HEAD_DIM)
    q16 = qrows.astype(BF16)
    s = jnp.concatenate(
        [_dot(q16, kp[0, 0].astype(BF16)) + bias_fn(p) for p, kp in enumerate(k_pages)], axis=1)
    kn = knew.astype(BF16).astype(F32)
    sn = jnp.sum(qrows * kn, axis=1, keepdims=True) + bias_new
    m = jnp.maximum(jnp.max(s, axis=1, keepdims=True), sn)
    pr = jnp.exp(s - m)
    pn = jnp.exp(sn - m)
    l = jnp.sum(pr, axis=1, keepdims=True) + pn
    acc = pn * vnew.astype(BF16).astype(F32)
    for p, vp in enumerate(v_pages):
        acc = acc + _dot_nt(pr[:, p * page:(p + 1) * page].astype(BF16), vp[0, 0].astype(BF16))
    return jnp.sum(jnp.where(keep, acc / l, 0.0), axis=0, keepdims=True)


def _decode_kernel(pt_ref, qa_ref, qb_ref, kan_ref, van_ref, kbn_ref, vbn_ref, fb_ref, dm_ref, *rest,
                   n_pages, page):
    fk = rest[0:n_pages]
    fv = rest[n_pages:2 * n_pages]
    dk = rest[2 * n_pages:3 * n_pages]
    dv = rest[3 * n_pages:4 * n_pages]
    oa_ref, ob_ref = rest[4 * n_pages:]
    oa = _decode_attend(qa_ref[0], kan_ref[0], van_ref[0], fk, fv,
                        lambda p: fb_ref[0, p], 0.0, FOX_HEADS)
    oa_ref[0] = oa.astype(BF16)
    ob = _decode_attend(qb_ref[0], kbn_ref[0], vbn_ref[0], dk, dv,
                        lambda p: dm_ref[0, :, p * page:(p + 1) * page],
                        dm_ref[0, :, n_pages * page:n_pages * page + 1], 8)
    ob_ref[0] = ob.astype(BF16)


def _decode_attention(page_table, qa3, qb3, kan, van, kbn, vbn, fbias, dmask,
                      cache_fk, cache_fv, cache_dk, cache_dv, layer):
    nb, n_pages = page_table.shape
    page = cache_fk.shape[3]
    wa = FOX_HEADS * HEAD_DIM
    wb = DSA_HEADS * HEAD_DIM

    def pspecs(width):
        return [pl.BlockSpec((1, 1, width, page), functools.partial(
            lambda b, pt, p: (layer, pt[b, p], 0, 0), p=p)) for p in range(n_pages)]

    vec = lambda w: pl.BlockSpec((1, 1, w), lambda b, pt: (b, 0, 0))
    gs = pltpu.PrefetchScalarGridSpec(
        num_scalar_prefetch=1,
        grid=(nb,),
        in_specs=[vec(wa), vec(wb), vec(wa), vec(wa), vec(wb), vec(wb),
                  pl.BlockSpec((1, n_pages, FOX_HEADS, page), lambda b, pt: (b, 0, 0, 0)),
                  vec(dmask.shape[2])] + pspecs(wa) + pspecs(wa) + pspecs(wb) + pspecs(wb),
        out_specs=[vec(wa), vec(wb)],
    )
    return pl.pallas_call(
        functools.partial(_decode_kernel, n_pages=n_pages, page=page),
        grid_spec=gs,
        out_shape=[jax.ShapeDtypeStruct((nb, 1, wa), BF16), jax.ShapeDtypeStruct((nb, 1, wb), BF16)],
        compiler_params=_cparams(("arbitrary",)),
        name="decode_attention",
    )(page_table, qa3, qb3, kan, van, kbn, vbn, fbias, dmask,
      *([cache_fk] * n_pages + [cache_fv] * n_pages + [cache_dk] * n_pages + [cache_dv] * n_pages))


def _prep_w_in(w_in):
    fw, dw = FOX_HEADS * HEAD_DIM, DSA_HEADS * HEAD_DIM
    o_fa = 3 * fw
    o_qb = o_fa + FOX_HEADS
    o_qi = o_qb + 3 * dw
    o_ki = o_qi + IDX_HEADS * IDX_DIM
    o_wi = o_ki + IDX_DIM
    o_u = o_wi + IDX_HEADS
    d = w_in.shape[0]
    misc = jnp.concatenate([w_in[:, o_ki:o_wi], w_in[:, o_fa:o_qb], w_in[:, o_wi:o_u],
                            jnp.zeros((d, LANES - IDX_DIM - FOX_HEADS - IDX_HEADS), w_in.dtype)], axis=1)
    return jnp.concatenate([w_in[:, 0:o_fa], w_in[:, o_qb:o_qi], w_in[:, o_qi:o_ki], misc,
                            w_in[:, o_u:]], axis=1).astype(BF16)


def _prep_w_in_t(w_in):
    fw, dw = FOX_HEADS * HEAD_DIM, DSA_HEADS * HEAD_DIM
    o_fa = 3 * fw
    o_qb = o_fa + FOX_HEADS
    o_kb = o_qb + dw
    o_qi = o_qb + 3 * dw
    o_ki = o_qi + IDX_HEADS * IDX_DIM
    o_wi = o_ki + IDX_DIM
    o_u = o_wi + IDX_HEADS
    d = w_in.shape[0]
    misc = jnp.concatenate([w_in[:, o_ki:o_wi], w_in[:, o_fa:o_qb], w_in[:, o_wi:o_u],
                            jnp.zeros((d, LANES - IDX_DIM - FOX_HEADS - IDX_HEADS), w_in.dtype)], axis=1)
    wq = jnp.concatenate([w_in[:, 0:fw], w_in[:, o_qb:o_kb], w_in[:, o_qi:o_ki], w_in[:, o_u:],
                          w_in[:, o_kb:o_kb + dw]] + [w_in[:, o_ki:o_wi]] * IDX_HEADS, axis=1).astype(BF16)
    wt = jnp.concatenate([w_in[:, fw:o_fa], w_in[:, o_kb:o_qi], misc], axis=1).T.astype(BF16)
    return wq, wt


def _block_diag(w_pool):
    g, c, _ = w_pool.shape
    out = jnp.zeros((g * c, g * c), w_pool.dtype)
    for i in range(g):
        out = out.at[i * c:(i + 1) * c, i * c:(i + 1) * c].set(w_pool[i])
    return out.astype(BF16)


def kernel(x_prompt, x_sample, cache_fox_k, cache_fox_v, cache_fox_logf, cache_dsa_k, cache_dsa_v,
           cache_dsa_kidx, state_pool, cache_mem_k, cache_mem_v, page_table, mem_prompt,
           norm_mix, w_in, b_forget, w_pool, pool_scale, w_out, norm_x, norm_mem, w_xq, w_xkv, w_xo,
           norm_ffn, ffn_w_gate, ffn_w_up, ffn_w_down, moe_router, moe_w_gate, moe_w_up, moe_w_down,
           norm_final):
    b, t, d = x_prompt.shape
    nb = x_sample.shape[0]
    depth = w_in.shape[0]
    n_pool, page = cache_fox_k.shape[1], cache_fox_k.shape[2]
    n_pages = page_table.shape[1]
    n_past = n_pages * page
    n_mem = mem_prompt.shape[1]
    fw, dw = FOX_HEADS * HEAD_DIM, DSA_HEADS * HEAD_DIM
    xw = X_HEADS * HEAD_DIM
    n = b * t

    ch, tq_fox, tq_dsa, tt_pool, tm_mix = 512, 512, 256, 512, 512
    tm_ffn, tf_dense, tm_moe, bm_moe, tf_moe, tc_moe = 1024, 256, 512, 512, 512, 256
    nt = t // ch
    gf = norm_final.reshape(1, d)

    pos_p = jnp.arange(t, dtype=I32)
    pos_s = jnp.full((nb,), n_past, I32)
    tab_p = (_rope_table(pos_p, HEAD_DIM, HEAD_DIM // 4, LANES), _rope_table(pos_p, IDX_DIM, IDX_DIM // 4, LANES),
             _rope_table_t(pos_p, HEAD_DIM, HEAD_DIM // 4), _rope_table_t(pos_p, IDX_DIM, IDX_DIM // 4))
    tab_s = (_rope_table(pos_s, HEAD_DIM, HEAD_DIM // 4, LANES), _rope_table(pos_s, IDX_DIM, IDX_DIM // 4, LANES),
             _rope_table(pos_s, IDX_DIM, IDX_DIM // 4, IDX_DIM))

    cfk = jnp.transpose(cache_fox_k, (0, 1, 3, 4, 2)).reshape(depth, n_pool, fw, page)
    cfv = jnp.transpose(cache_fox_v, (0, 1, 3, 4, 2)).reshape(depth, n_pool, fw, page)
    cdk = jnp.transpose(cache_dsa_k, (0, 1, 3, 4, 2)).reshape(depth, n_pool, dw, page)
    cdv = jnp.transpose(cache_dsa_v, (0, 1, 3, 4, 2)).reshape(depth, n_pool, dw, page)
    ckidx = jnp.transpose(cache_dsa_kidx, (0, 1, 3, 2))
    clogf = jnp.transpose(cache_fox_logf, (0, 1, 3, 2))
    cmk = jnp.transpose(cache_mem_k, (0, 1, 3, 4, 2)).reshape(depth, nb, xw, n_mem)
    cmv = jnp.transpose(cache_mem_v, (0, 1, 3, 4, 2)).reshape(depth, nb, xw, n_mem)
    pool_t = jnp.transpose(state_pool, (0, 2, 1, 3))

    xp = x_prompt.reshape(n, d)
    xs = x_sample.reshape(nb, d)
    st_p, st_s = [], []
    for l in range(depth):
        wp = _prep_w_in(w_in[l])
        wq, wt = _prep_w_in_t(w_in[l])
        bfp = jnp.zeros((1, LANES), F32).at[0, _MISC_LF:_MISC_WI].set(b_forget[l].astype(F32))
        g_mix = norm_mix[l].reshape(1, d)
        g_x = norm_x[l].reshape(1, d)
        g_ffn = norm_ffn[l].reshape(1, d)
        wblk = _block_diag(w_pool[l])
        pscale = pool_scale[l].reshape(1, -1).astype(F32)
        wo16 = w_out[l].astype(BF16)
        wxq16 = w_xq[l].astype(BF16)
        wxo16 = w_xo[l].astype(BF16)

        (qa16, qb16, qi16, u, kb16, kir16, kat, vat, kbt, vbt, misct, ka16t, va16t, vb16t) = _proj_t(
            xp, g_mix, wq, wt, *tab_p, bfp.reshape(LANES, 1), b, t, ch)
        lft = misct[:, _MISC_LF:_MISC_WI, :]
        cft4 = jnp.cumsum(lft, axis=2).reshape(b, FOX_HEADS, nt, ch)
        oa = _fox_prompt(qa16, ka16t, va16t, cft4, b, t, tq_fox, ch)
        ob = _dsa_prompt(qi16, misct, kir16, qb16, kb16, vb16t, b, t, tq_dsa, ch)
        oc = _pool_prompt(u, wblk, pscale, b, t, tt_pool)
        kvt, kvt16 = _norm_matmul_t(mem_prompt.reshape(b * n_mem, d), norm_mem[l].reshape(1, d),
                                    w_xkv[l].T.astype(BF16), b, n_mem)
        xp = _mixx_prompt(xp, oa, ob, oc, wo16, g_x, wxq16, kvt16[:, :xw], kvt16[:, xw:], wxo16, t, tm_mix)
        st_p.append((kat, vat, lft, kbt, vbt, misct[:, _MISC_KI:_MISC_KI + IDX_DIM, :],
                     u.reshape(b, t, -1)[:, -POOL_BUF:], kvt[:, :xw], kvt[:, xw:]))

        (ka_s, va_s, kb_s, vb_s, misc_s, u_s, qa16s, _, _, qb16s, _, _, qi16s) = _proj(
            xs, g_mix, wp, *tab_s, bfp, nb, 1)
        lf_s = misc_s[:, _MISC_LF:_MISC_WI]
        ki_s = misc_s[:, _MISC_KI:_MISC_KI + IDX_DIM]
        wi_s = misc_s[:, _MISC_WI:_MISC_WI + IDX_HEADS]
        lf_past = jnp.transpose(clogf[l][page_table], (0, 2, 1, 3)).reshape(nb, FOX_HEADS, n_past).astype(F32)
        csum = jnp.cumsum(lf_past, axis=2)
        fbias = lf_s[:, :, None] + (csum[:, :, -1:] - csum)
        fbias = jnp.transpose(fbias.reshape(nb, FOX_HEADS, n_pages, page), (0, 2, 1, 3))
        scores = _dsa_score_sample(page_table, qi16s.reshape(nb, IDX_HEADS, IDX_DIM),
                                   wi_s.reshape(nb, IDX_HEADS, 1), ki_s.reshape(nb, 1, IDX_DIM), ckidx, l)
        scores_t = jnp.transpose(scores.reshape(nb, n_pages + 1, page), (1, 2, 0))
        dmask = _dsa_select_sample(scores_t, n_past + 1, min(TOPK_MAX, (n_past + 1) // 4))
        dmask = jnp.transpose(dmask, (2, 0, 1))
        oa_s, ob_s = _decode_attention(
            page_table, qa16s.reshape(nb, 1, fw), qb16s.reshape(nb, 1, dw),
            ka_s.reshape(nb, 1, fw), va_s.reshape(nb, 1, fw), kb_s.reshape(nb, 1, dw),
            vb_s.reshape(nb, 1, dw), fbias, dmask.reshape(nb, 1, -1), cfk, cfv, cdk, cdv, l)
        oc_s = _pool_sample(pool_t[l], u_s, wblk, pscale)
        xs = _mixx_sample(xs, oa_s.reshape(nb, fw), ob_s.reshape(nb, dw), oc_s, wo16, g_x, wxq16,
                          cmk, cmv, l, wxo16, 8)
        st_s.append((ka_s.reshape(nb, 1, FOX_HEADS, HEAD_DIM), va_s.reshape(nb, 1, FOX_HEADS, HEAD_DIM),
                     lf_s.reshape(nb, 1, FOX_HEADS), kb_s.reshape(nb, 1, DSA_HEADS, HEAD_DIM),
                     vb_s.reshape(nb, 1, DSA_HEADS, HEAD_DIM), ki_s.reshape(nb, 1, IDX_DIM),
                     jnp.concatenate([pool_t[l][1:], u_s[None]], axis=0)))

        i = l // 2
        last = l == depth - 1
        if l % 2 == 0:
            wg16 = ffn_w_gate[i:i + 1].astype(BF16)
            wu16 = ffn_w_up[i:i + 1].astype(BF16)
            wd16 = ffn_w_down[i:i + 1].astype(BF16)
            xp = _ffn_dense(xp, g_ffn, wg16, wu16, wd16, tm_ffn, tf_dense)
            xs = _ffn_dense(xs, g_ffn, wg16, wu16, wd16, nb, tf_dense)
            if last:
                xp = _rms_out(xp, gf, tm_ffn)
                xs = _rms_out(xs, gf, nb)
        else:
            wr16 = jnp.zeros((d, LANES), F32).at[:, :N_EXPERTS].set(moe_router[i].astype(F32))
            wg, wu, wd = (w[i].astype(BF16) for w in (moe_w_gate, moe_w_up, moe_w_down))
            xp = _moe(xp, g_ffn, wr16, wg, wu, wd, gf, tm_moe, bm_moe, tf_moe, tc_moe, last)
            xs = _moe(xs, g_ffn, wr16, wg, wu, wd, gf, nb, 128, tf_moe, nb, last)

    y_prompt = xp.reshape(b, t, d)
    y_sample = xs.reshape(nb, 1, d)
    stack = lambda states, j: jnp.stack([s[j] for s in states], axis=0)

    def heads_last(a, heads):
        dp, bb, _, pp = a.shape
        return jnp.transpose(a.reshape(dp, bb, heads, HEAD_DIM, pp), (0, 1, 4, 2, 3))

    swap = lambda a: jnp.transpose(a, (0, 1, 3, 2))
    outs_p = (heads_last(stack(st_p, 0), FOX_HEADS), heads_last(stack(st_p, 1), FOX_HEADS),
              swap(stack(st_p, 2)), heads_last(stack(st_p, 3), DSA_HEADS),
              heads_last(stack(st_p, 4), DSA_HEADS), swap(stack(st_p, 5)), stack(st_p, 6),
              heads_last(stack(st_p, 7), X_HEADS), heads_last(stack(st_p, 8), X_HEADS))
    outs_s = tuple(stack(st_s, j) for j in range(6)) + (jnp.transpose(stack(st_s, 6), (0, 2, 1, 3)),)
    return (y_prompt, y_sample) + outs_p + outs_s
```
